```python
import math
import jax, jax.numpy as jnp
from jax import lax
import numpy as np

D_MODEL = 1024
BATCH = 8
SEQ = 4096
DEPTH = 4

SB_HEADS = 4
SB_DIM = D_MODEL // (4 * SB_HEADS)
GDN_HEADS = 4
GDN_DK = D_MODEL // (2 * GDN_HEADS)
GDN_DV = D_MODEL // (2 * GDN_HEADS)
DA_HEADS = 4
DA_V = D_MODEL // (4 * DA_HEADS)
DA_QK = DA_V // 2
SB_W = SB_HEADS * SB_DIM
GDN_KW = GDN_HEADS * GDN_DK
GDN_W = GDN_HEADS * GDN_DV
DA_W = DA_HEADS * DA_V
MIX_W = SB_W + GDN_W + DA_W
IN_SPLIT_SIZES = (SB_W, SB_W, SB_W, GDN_KW, GDN_KW, GDN_W, GDN_W, GDN_HEADS, GDN_HEADS, DA_W, DA_W, DA_W)
IN_WIDTH = sum(IN_SPLIT_SIZES)
IN_SPLIT_POINTS = tuple(int(p) for p in np.cumsum(IN_SPLIT_SIZES)[:-1])

Q_BLOCK = 128
GDN_CHUNK = 64
CONV_K = 4
CONV_CH = 2 * GDN_KW + GDN_W
ROPE_THETA = 500000.0
ROT_DIM = DA_QK // 4

N_EXPERTS = 32
TOP_K = 4
D_FF = D_MODEL
SWIGLU_ALPHA = 1.702
SWIGLU_LIMIT = 7.0
MOE_BLOCK = 256

DN_ALPHA = (2 * DEPTH) ** 0.25
DN_BETA = (8 * DEPTH) ** -0.25
LN_EPS = 1e-5
RMS_EPS = 1e-6

kernel_name = 'hybrid_sb_gdn_diff_moe_deepnorm_adaln'


def layer_norm(x, g, b):
    xf = x.astype(jnp.float32)
    mu = jnp.mean(xf, -1, keepdims=True)
    var = jnp.mean(jnp.square(xf - mu), -1, keepdims=True)
    return ((xf - mu) * lax.rsqrt(var + LN_EPS) * g + b).astype(x.dtype)


def rms_norm(x, w):
    xf = x.astype(jnp.float32)
    return (xf * lax.rsqrt(jnp.mean(xf * xf, -1, keepdims=True) + RMS_EPS) * w).astype(x.dtype)


def l2_normalize(x):
    xf = x.astype(jnp.float32)
    return xf * lax.rsqrt(jnp.sum(xf * xf, -1, keepdims=True) + 1e-6)


def split_heads(t, n):
    B, S, _ = t.shape
    return t.reshape(B, S, n, -1).transpose(0, 2, 1, 3)


def merge_heads(t):
    B, H, S, d = t.shape
    return t.transpose(0, 2, 1, 3).reshape(B, S, H * d)


def sweep_query_blocks(fn, q):
    B, H, S = q.shape[:3]
    nb = S // Q_BLOCK
    qb = jnp.moveaxis(q.reshape(B, H, nb, Q_BLOCK, *q.shape[3:]), 2, 0)
    starts = jnp.arange(nb, dtype=jnp.int32) * Q_BLOCK
    out = lax.map(lambda a: fn(a[0], a[1]), (qb, starts))
    return jnp.moveaxis(out, 0, 2).reshape(B, H, S, -1)


def stick_breaking_attention(q, k, v):
    S = q.shape[2]
    scale = q.shape[-1] ** -0.5
    key_pos = jnp.arange(S)

    def block(qb, start):
        z = jnp.einsum('bhqd,bhkd->bhqk', qb, k).astype(jnp.float32) * scale
        q_pos = start + jnp.arange(Q_BLOCK)
        causal = key_pos[None, :] < q_pos[:, None]
        log_keep = jnp.where(causal, jax.nn.log_sigmoid(-z), 0.0)
        between = lax.cumsum(log_keep, axis=3, reverse=True) - log_keep
        w = jnp.where(causal, jnp.exp(jax.nn.log_sigmoid(z) + between), 0.0)
        return jnp.einsum('bhqk,bhkd->bhqd', w.astype(v.dtype), v)

    return sweep_query_blocks(block, q)


def causal_depthwise_conv(x, w):
    K = w.shape[0]
    return lax.conv_general_dilated(x, w[:, None, :], window_strides=(1,), padding=[(K - 1, 0)],
                                    dimension_numbers=('NWC', 'WIO', 'NWC'), feature_group_count=x.shape[-1])


def gated_delta_rule(q, k, v, g, beta):
    f32 = jnp.float32
    B, H, S, dk = k.shape
    dv = v.shape[-1]
    C = GDN_CHUNK
    n = S // C
    q = (q.astype(f32) * dk ** -0.5).reshape(B, H, n, C, dk)
    k = k.astype(f32).reshape(B, H, n, C, dk)
    v = v.astype(f32).reshape(B, H, n, C, dv)
    beta = beta.astype(f32).reshape(B, H, n, C)
    g = jnp.cumsum(g.astype(f32).reshape(B, H, n, C), axis=-1)
    lower = jnp.tril(jnp.ones((C, C), bool))
    strict = jnp.tril(jnp.ones((C, C), bool), -1)
    decay = jnp.where(lower, jnp.exp(jnp.where(lower, g[..., :, None] - g[..., None, :], 0.0)), 0.0)
    k_beta = k * beta[..., None]
    m = jnp.where(strict, jnp.einsum('bhnid,bhnjd->bhnij', k_beta, k) * decay, 0.0)
    rhs = jnp.concatenate([v * beta[..., None], k_beta * jnp.exp(g)[..., None]], -1)
    sol = lax.linalg.triangular_solve(m + jnp.eye(C, dtype=f32), rhs, left_side=True, lower=True,
                                      unit_diagonal=True)
    u = sol[..., :dv]
    w = sol[..., dv:]
    attn_intra = jnp.where(lower, jnp.einsum('bhnid,bhnjd->bhnij', q, k) * decay, 0.0)

    def step(state, inp):
        q_i, k_i, u_i, w_i, g_i, a_i = inp
        v_new = u_i - jnp.einsum('bhcd,bhde->bhce', w_i, state)
        o = (jnp.einsum('bhcd,bhde->bhce', q_i * jnp.exp(g_i)[..., None], state)
             + jnp.einsum('bhij,bhje->bhie', a_i, v_new))
        g_last = g_i[..., -1]
        state = (state * jnp.exp(g_last)[..., None, None]
                 + jnp.einsum('bhcd,bhce->bhde', k_i * jnp.exp(g_last[..., None] - g_i)[..., None], v_new))
        return state, o

    xs = tuple(jnp.moveaxis(t, 2, 0) for t in (q, k, u, w, g, attn_intra))
    _, o = lax.scan(step, jnp.zeros((B, H, dk, dv), f32), xs)
    return jnp.moveaxis(o, 0, 2).reshape(B, H, S, dv)


def partial_rotary(x, cos, sin):
    c = cos[None, :, None, None, :].astype(x.dtype)
    s = sin[None, :, None, None, :].astype(x.dtype)
    half = ROT_DIM // 2
    x1, x2, xp = x[..., :half], x[..., half:ROT_DIM], x[..., ROT_DIM:]
    return jnp.concatenate([x1 * c - x2 * s, x2 * c + x1 * s, xp], -1)


def differential_attention(q, k, v, lam, subln_w, lambda_init):
    S = q.shape[2]
    scale = q.shape[-1] ** -0.5
    key_pos = jnp.arange(S)

    def block(qb, start):
        s = jnp.einsum('bhqcd,bhkcd->bhcqk', qb, k).astype(jnp.float32) * scale
        q_pos = start + jnp.arange(Q_BLOCK)
        causal = key_pos[None, :] <= q_pos[:, None]
        p = jax.nn.softmax(jnp.where(causal, s, -jnp.inf), axis=-1)
        w = p[:, :, 0] - lam * p[:, :, 1]
        return jnp.einsum('bhqk,bhkd->bhqd', w.astype(v.dtype), v)

    o = sweep_query_blocks(block, q)
    return rms_norm(o, subln_w) * (1.0 - lambda_init)


def hybrid_mixer(h, w_in, conv_w, a_log, dt_bias, gdn_norm_w, lam_p, subln_w, w_out, cos, sin, lambda_init):
    B, S, _ = h.shape
    f32 = jnp.float32
    proj = h @ w_in
    sb_q, sb_k, sb_v, gq, gk, gv, gz, ga, gb, dq, dk, dv = jnp.split(proj, IN_SPLIT_POINTS, axis=-1)

    o_sb = merge_heads(stick_breaking_attention(split_heads(sb_q, SB_HEADS), split_heads(sb_k, SB_HEADS),
                                                split_heads(sb_v, SB_HEADS)))

    qkv = jax.nn.silu(causal_depthwise_conv(jnp.concatenate([gq, gk, gv], -1), conv_w))
    gq, gk, gv = jnp.split(qkv, (GDN_KW, 2 * GDN_KW), axis=-1)
    decay = -jnp.exp(a_log.astype(f32))[None, :, None] * jax.nn.softplus(
        ga.astype(f32).transpose(0, 2, 1) + dt_bias.astype(f32)[None, :, None])
    beta = jax.nn.sigmoid(gb.astype(f32)).transpose(0, 2, 1)
    o_g = gated_delta_rule(l2_normalize(split_heads(gq, GDN_HEADS)), l2_normalize(split_heads(gk, GDN_HEADS)),
                           split_heads(gv, GDN_HEADS), decay, beta)
    o_g = rms_norm(o_g, gdn_norm_w) * jax.nn.silu(split_heads(gz, GDN_HEADS).astype(f32))
    o_gdn = merge_heads(o_g).astype(h.dtype)

    dq = partial_rotary(dq.reshape(B, S, DA_HEADS, 2, DA_QK), cos, sin).transpose(0, 2, 1, 3, 4)
    dk = partial_rotary(dk.reshape(B, S, DA_HEADS, 2, DA_QK), cos, sin).transpose(0, 2, 1, 3, 4)
    lp = lam_p.astype(f32)
    lam = jnp.exp(jnp.sum(lp[0] * lp[1])) - jnp.exp(jnp.sum(lp[2] * lp[3])) + lambda_init
    o_da = merge_heads(differential_attention(dq, dk, split_heads(dv, DA_HEADS), lam, subln_w, lambda_init))

    mixed = jnp.concatenate([o_sb, o_gdn, o_da.astype(h.dtype)], -1)
    return mixed @ w_out


def clamped_swiglu(hgu):
    x_glu = jnp.minimum(hgu[..., ::2], SWIGLU_LIMIT)
    x_lin = jnp.clip(hgu[..., 1::2], -SWIGLU_LIMIT, SWIGLU_LIMIT)
    return x_glu * jax.nn.sigmoid(SWIGLU_ALPHA * x_glu) * (x_lin + 1.0)


def moe_ffn(h, w_router, b_router, w_gu, b_gu, w_down, b_down):
    B, S, D = h.shape
    N = B * S
    xf = h.reshape(N, D)
    logits = (xf @ w_router).astype(jnp.float32) + b_router.astype(jnp.float32)
    top_val, top_idx = lax.top_k(logits, TOP_K)
    gates = jax.nn.softmax(top_val, axis=-1)
    e_flat = top_idx.reshape(-1)
    tok_flat = jnp.repeat(jnp.arange(N, dtype=jnp.int32), TOP_K)
    gate_flat = gates.reshape(-1)
    order = jnp.argsort(e_flat)
    e_sorted = e_flat[order]
    counts = jnp.zeros((N_EXPERTS,), jnp.int32).at[e_flat].add(1)
    starts = jnp.cumsum(counts) - counts
    padded = (counts + MOE_BLOCK - 1) // MOE_BLOCK * MOE_BLOCK
    padded_ends = jnp.cumsum(padded)
    padded_starts = padded_ends - padded
    rank = jnp.arange(N * TOP_K, dtype=jnp.int32)
    dest = padded_starts[e_sorted] + rank - starts[e_sorted]
    n_rows = -(-(N * TOP_K) // MOE_BLOCK) * MOE_BLOCK + N_EXPERTS * MOE_BLOCK
    n_blocks = n_rows // MOE_BLOCK
    row_tok = jnp.full((n_rows,), N, jnp.int32).at[dest].set(tok_flat[order])
    row_gate = jnp.zeros((n_rows,), jnp.float32).at[dest].set(gate_flat[order])
    block_expert = jnp.minimum(jnp.searchsorted(padded_ends, jnp.arange(n_blocks, dtype=jnp.int32) * MOE_BLOCK,
                                                side='right'), N_EXPERTS - 1)
    x_rows = jnp.concatenate([xf, jnp.zeros((1, D), xf.dtype)], 0)[row_tok].reshape(n_blocks, MOE_BLOCK, D)

    def expert_block(args):
        xb, e = args
        return clamped_swiglu(xb @ w_gu[e] + b_gu[e]) @ w_down[e] + b_down[e]

    y_rows = lax.map(expert_block, (x_rows, block_expert)).reshape(n_rows, D)
    y = jax.ops.segment_sum(y_rows * row_gate[:, None].astype(y_rows.dtype), row_tok, num_segments=N + 1)[:N]
    return y.reshape(B, S, D)


def setup_inputs(seed: int = 0) -> dict:
    key = jax.random.key(seed)
    ks = jax.random.split(key, 24)
    f32 = jnp.float32
    D = D_MODEL

    def nrm(k, shape, scale):
        return jax.random.normal(k, shape, f32) * scale

    dt = jnp.exp(jax.random.uniform(ks[7], (DEPTH, GDN_HEADS), f32, math.log(1e-3), math.log(1e-1)))
    return {
        'x': nrm(ks[0], (BATCH, SEQ, D), 1.0),
        'c': nrm(ks[1], (BATCH, D), 1.0),
        'w_ada': nrm(ks[2], (DEPTH, D, 6 * D), 0.1 * D ** -0.5),
        'b_ada': nrm(ks[3], (DEPTH, 6 * D), 0.02),
        'w_in': nrm(ks[4], (DEPTH, D, IN_WIDTH), D ** -0.5),
        'conv_w': nrm(ks[5], (DEPTH, CONV_K, CONV_CH), CONV_K ** -0.5),
        'gdn_a_log': jnp.log(jax.random.uniform(ks[6], (DEPTH, GDN_HEADS), f32, 1.0, 16.0)),
        'gdn_dt_bias': dt + jnp.log(-jnp.expm1(-dt)),
        'gdn_norm_w': 1.0 + nrm(ks[8], (DEPTH, GDN_DV), 0.02),
        'da_lambda': nrm(ks[9], (DEPTH, 4, DA_QK), 0.1),
        'da_subln_w': 1.0 + nrm(ks[10], (DEPTH, DA_V), 0.02),
        'w_out': nrm(ks[11], (DEPTH, MIX_W, D), DN_BETA * MIX_W ** -0.5),
        'ln1_g': 1.0 + nrm(ks[12], (DEPTH, D), 0.02),
        'ln1_b': nrm(ks[13], (DEPTH, D), 0.02),
        'w_router': nrm(ks[14], (DEPTH, D, N_EXPERTS), D ** -0.5),
        'b_router': nrm(ks[15], (DEPTH, N_EXPERTS), 0.01),
        'w_gu': nrm(ks[16], (DEPTH, N_EXPERTS, D, 2 * D_FF), D ** -0.5),
        'b_gu': nrm(ks[17], (DEPTH, N_EXPERTS, 2 * D_FF), 0.02),
        'w_down': nrm(ks[18], (DEPTH, N_EXPERTS, D_FF, D), DN_BETA * D_FF ** -0.5),
        'b_down': nrm(ks[19], (DEPTH, N_EXPERTS, D), 0.02),
        'ln2_g': 1.0 + nrm(ks[20], (DEPTH, D), 0.02),
        'ln2_b': nrm(ks[21], (DEPTH, D), 0.02),
    }


def reference(x, c, w_ada, b_ada, w_in, conv_w, gdn_a_log, gdn_dt_bias, gdn_norm_w, da_lambda, da_subln_w,
              w_out, ln1_g, ln1_b, w_router, b_router, w_gu, b_gu, w_down, b_down, ln2_g, ln2_b):
    S = x.shape[1]
    pos = jnp.arange(S, dtype=jnp.float32)
    inv_freq = ROPE_THETA ** (-jnp.arange(0, ROT_DIM, 2, dtype=jnp.float32) / ROT_DIM)
    ang = pos[:, None] * inv_freq[None, :]
    cos, sin = jnp.cos(ang), jnp.sin(ang)
    cond = jax.nn.silu(c)
    for l in range(DEPTH):
        mod = cond @ w_ada[l] + b_ada[l]
        sh1, sc1, g1, sh2, sc2, g2 = [m[:, None, :] for m in jnp.split(mod, 6, axis=-1)]
        lambda_init = 0.8 - 0.6 * math.exp(-0.3 * l)
        h = x * (1.0 + sc1) + sh1
        mix = hybrid_mixer(h, w_in[l], conv_w[l], gdn_a_log[l], gdn_dt_bias[l], gdn_norm_w[l], da_lambda[l],
                           da_subln_w[l], w_out[l], cos, sin, lambda_init)
        x = layer_norm(DN_ALPHA * x + (1.0 + g1) * mix, ln1_g[l], ln1_b[l])
        h = x * (1.0 + sc2) + sh2
        ffn = moe_ffn(h, w_router[l], b_router[l], w_gu[l], b_gu[l], w_down[l], b_down[l])
        x = layer_norm(DN_ALPHA * x + (1.0 + g2) * ffn, ln2_g[l], ln2_b[l])
    return x
```

```python
import functools
import math

import jax
import jax.numpy as jnp
from jax import lax
from jax.experimental import pallas as pl
from jax.experimental.pallas import tpu as pltpu

F32 = jnp.float32
BF16 = jnp.bfloat16
I32 = jnp.int32

LANES = 128
VMEM_LIMIT = 56 * 1024 * 1024

D_MODEL = 1024
SB_HEADS = 4
SB_DIM = 64
GDN_HEADS = 4
GDN_DK = 128
GDN_DV = 128
DA_HEADS = 4
DA_V = 64
DA_QK = 32
ROT_DIM = DA_QK // 4
ROPE_THETA = 500000.0
SB_W = SB_HEADS * SB_DIM
GDN_KW = GDN_HEADS * GDN_DK
GDN_W = GDN_HEADS * GDN_DV
DA_W = DA_HEADS * DA_V
CONV_K = 4
N_EXPERTS = 32
TOP_K = 4
D_FF = D_MODEL
SWIGLU_ALPHA = 1.702
SWIGLU_LIMIT = 7.0
MOE_BLOCK = 256
LN_EPS = 1e-5
RMS_EPS = 1e-6
NEG_BIG = -1e30

C_SB = 0
C_G4 = 3 * SB_W
C_DQK = C_G4 + 4 * GDN_W
C_DV = C_DQK + 2 * DA_W
W_MAIN = C_DV + DA_W


def _dot(a, b):
    return jnp.dot(a, b, preferred_element_type=F32)


def _dot_nt(a, b):
    return lax.dot_general(a, b, (((1,), (1,)), ((), ())), preferred_element_type=F32)


def _dot_tn(a, b):
    return lax.dot_general(a, b, (((0,), (0,)), ((), ())), preferred_element_type=F32)


def _split3(x):
    h1 = x.astype(BF16)
    r1 = x - h1.astype(F32)
    h2 = r1.astype(BF16)
    h3 = (r1 - h2.astype(F32)).astype(BF16)
    return h1, h2, h3


def _mm(a, b, passes):
    ah = a.astype(BF16)
    bh = b.astype(BF16)
    if passes == 1:
        return _dot(ah, bh)
    al = (a - ah.astype(F32)).astype(BF16)
    bl = (b - bh.astype(F32)).astype(BF16)
    return _dot(ah, bh) + (_dot(al, bh) + _dot(ah, bl))


def _silu(x):
    return x * jax.nn.sigmoid(x)


def _softplus(x):
    return jnp.maximum(x, 0.0) + jnp.log(1.0 + jnp.exp(-jnp.abs(x)))


def _cparams(sem):
    return pltpu.CompilerParams(dimension_semantics=sem, vmem_limit_bytes=VMEM_LIMIT)


def _ada_kernel(c_ref, w_ref, b_ref, o_ref):
    cond = _silu(c_ref[...])
    w = w_ref[0]
    c1, c2, c3 = _split3(cond)
    w1, w2, w3 = _split3(w)
    acc = _dot(c1, w1) + (_dot(c1, w2) + _dot(c2, w1)) + (_dot(c2, w2) + _dot(c1, w3) + _dot(c3, w1))
    o_ref[0] = acc + b_ref[0]


def _ada_call(c, w_ada, b_ada):
    depth, d, d6 = w_ada.shape
    b = c.shape[0]
    nj = d6 // d
    return pl.pallas_call(
        _ada_kernel,
        out_shape=jax.ShapeDtypeStruct((depth, b, d6), F32),
        grid=(depth, nj),
        in_specs=[
            pl.BlockSpec((b, d), lambda l, j: (0, 0)),
            pl.BlockSpec((1, d, d), lambda l, j: (l, 0, j)),
            pl.BlockSpec((1, 1, d), lambda l, j: (l, 0, j)),
        ],
        out_specs=pl.BlockSpec((1, b, d), lambda l, j: (l, 0, j)),
        compiler_params=_cparams(("parallel", "parallel")),
        name="ada_mod",
    )(c, w_ada, b_ada.reshape(depth, 1, d6))


def _inproj_kernel(x_ref, sc_ref, sh_ref, w_ref, wab_ref, wabt_ref, cos_ref, sinm_ref, sinp_ref,
                   sbq_ref, sbk_ref, sbv_ref, g4_ref, daq_ref, dak_ref, dav_ref, gab_ref, gabt_ref):
    h = (x_ref[...] * (1.0 + sc_ref[0]) + sh_ref[0]).astype(BF16)

    def seg(a, n):
        return _dot(h, w_ref[:, a:a + n])

    sbq_ref[...] = (seg(C_SB, SB_W) * (SB_DIM ** -0.5)).astype(BF16)
    sbk_ref[...] = seg(C_SB + SB_W, SB_W).astype(BF16)
    sbv_ref[...] = seg(C_SB + 2 * SB_W, SB_W).astype(BF16)
    for j in range(4):
        g4_ref[:, j * GDN_W:(j + 1) * GDN_W] = seg(C_G4 + j * GDN_W, GDN_W)
    cos = cos_ref[...]
    sinm = sinm_ref[...]
    sinp = sinp_ref[...]
    for ref, base in ((daq_ref, C_DQK), (dak_ref, C_DQK + DA_W)):
        for j in range(DA_W // LANES):
            a = seg(base + j * LANES, LANES)
            r = a * cos + pltpu.roll(a, LANES - ROT_DIM // 2, 1) * sinm + pltpu.roll(a, ROT_DIM // 2, 1) * sinp
            ref[:, j * LANES:(j + 1) * LANES] = r.astype(BF16)
    dav_ref[...] = seg(C_DV, DA_W).astype(BF16)
    gab_ref[...] = _dot(h, wab_ref[...])
    gabt_ref[...] = _dot_nt(wabt_ref[...], h)


def _inproj_call(x2, sc, sh, w_main, w_ab, w_abt, cos_t, sinm_t, sinp_t, *, seq, tm):
    n, d = x2.shape
    nblk_s = seq // tm
    row = lambda i: (i, 0)
    bat = lambda i: (i // nblk_s, 0, 0)
    pos = lambda i: (i % nblk_s, 0)
    const = lambda i: (0, 0)
    outs = [
        (SB_W, BF16), (SB_W, BF16), (SB_W, BF16), (4 * GDN_W, F32),
        (DA_W, BF16), (DA_W, BF16), (DA_W, BF16), (LANES, F32),
    ]
    out_shape = [jax.ShapeDtypeStruct((n, w), dt) for w, dt in outs] + [jax.ShapeDtypeStruct((8, n), F32)]
    out_specs = [pl.BlockSpec((tm, w), row) for w, _ in outs] + [pl.BlockSpec((8, tm), lambda i: (0, i))]
    return pl.pallas_call(
        _inproj_kernel,
        out_shape=out_shape,
        grid=(n // tm,),
        in_specs=[
            pl.BlockSpec((tm, d), row),
            pl.BlockSpec((1, 1, d), bat),
            pl.BlockSpec((1, 1, d), bat),
            pl.BlockSpec(w_main.shape, const),
            pl.BlockSpec(w_ab.shape, const),
            pl.BlockSpec(w_abt.shape, const),
            pl.BlockSpec((tm, LANES), pos),
            pl.BlockSpec((tm, LANES), pos),
            pl.BlockSpec((tm, LANES), pos),
        ],
        out_specs=out_specs,
        compiler_params=_cparams(("parallel",)),
        name="inproj",
    )(x2, sc, sh, w_main, w_ab, w_abt, cos_t, sinm_t, sinp_t)


def _sb_kernel(q_ref, k_ref, v_ref, o_ref, *, tq):
    i = pl.program_id(2)
    q = q_ref[...]
    lane = lax.broadcasted_iota(I32, (1, LANES), 1)
    rows = lax.broadcasted_iota(I32, (tq, tq), 0)
    cols = lax.broadcasted_iota(I32, (tq, tq), 1)
    tri = cols < rows
    u = jnp.where(tri, 1.0, 0.0).astype(BF16)
    outs = []
    for hh in range(2):
        hmask = (lane >= SB_DIM * hh) & (lane < SB_DIM * (hh + 1))
        qh = jnp.where(hmask, q, jnp.zeros_like(q))

        def blk(j, carry, acc, masked, qh=qh):
            start = pl.multiple_of(j * tq, tq)
            kb = k_ref[pl.ds(start, tq), :]
            vb = v_ref[pl.ds(start, tq), :]
            z = _dot_nt(qh, kb)
            sp = _softplus(z)
            lk = -sp
            if masked:
                lk = jnp.where(tri, lk, 0.0)
            hi = lk.astype(BF16)
            lo = (lk - hi.astype(F32)).astype(BF16)
            between = _dot(hi, u) + _dot(lo, u)
            w = jnp.exp(z - sp + between + carry)
            if masked:
                w = jnp.where(tri, w, 0.0)
            acc = acc + _dot(w.astype(BF16), vb)
            carry = carry + jnp.sum(lk, axis=1, keepdims=True)
            return carry, acc

        carry, acc = blk(i, jnp.zeros((tq, 1), F32), jnp.zeros((tq, LANES), F32), True)

        def body(jj, ca, blk=blk):
            return blk(i - 1 - jj, ca[0], ca[1], False)

        carry, acc = lax.fori_loop(0, i, body, (carry, acc))
        outs.append(acc)
    o_ref[...] = jnp.where(lane < SB_DIM, outs[0], outs[1]).astype(BF16)


def _sb_call(q, k, v, *, batch, seq, tq):
    n = q.shape[0]
    nq = seq // tq
    return pl.pallas_call(
        functools.partial(_sb_kernel, tq=tq),
        out_shape=jax.ShapeDtypeStruct((n, SB_W), BF16),
        grid=(batch, SB_W // LANES, nq),
        in_specs=[
            pl.BlockSpec((tq, LANES), lambda b, p, i: (b * nq + i, p)),
            pl.BlockSpec((seq, LANES), lambda b, p, i: (b, p)),
            pl.BlockSpec((seq, LANES), lambda b, p, i: (b, p)),
        ],
        out_specs=pl.BlockSpec((tq, LANES), lambda b, p, i: (b * nq + i, p)),
        compiler_params=_cparams(("parallel", "parallel", "arbitrary")),
        name="sb_attn",
    )(q, k, v)


def _da_kernel(lam_ref, q_ref, k_ref, v_ref, subw_ref, o_ref, *, tq, lambda_init):
    i = pl.program_id(2)
    lp = lam_ref[...]
    lam = (jnp.exp(jnp.sum(lp[0:1] * lp[1:2], axis=1, keepdims=True))
           - jnp.exp(jnp.sum(lp[2:3] * lp[3:4], axis=1, keepdims=True)) + lambda_init)
    q = q_ref[...]
    lane = lax.broadcasted_iota(I32, (1, LANES), 1)
    rows = lax.broadcasted_iota(I32, (tq, tq), 0)
    cols = lax.broadcasted_iota(I32, (tq, tq), 1)
    tri = cols <= rows
    scale = DA_QK ** -0.5
    heads = []
    for hh in range(2):
        maps = []
        for c in range(2):
            g = 2 * hh + c
            gmask = (lane >= DA_QK * g) & (lane < DA_QK * (g + 1))
            qm = jnp.where(gmask, q, jnp.zeros_like(q))

            def blk(j, m, l, acc, masked, qm=qm):
                start = pl.multiple_of(j * tq, tq)
                kb = k_ref[pl.ds(start, tq), :]
                vb = v_ref[pl.ds(start, tq), :]
                s = _dot_nt(qm, kb) * scale
                if masked:
                    s = jnp.where(tri, s, NEG_BIG)
                m_new = jnp.maximum(m, jnp.max(s, axis=1, keepdims=True))
                alpha = jnp.exp(m - m_new)
                p = jnp.exp(s - m_new)
                l = alpha * l + jnp.sum(p, axis=1, keepdims=True)
                acc = alpha * acc + _dot(p.astype(BF16), vb)
                return m_new, l, acc

            m, l, acc = blk(i, jnp.full((tq, 1), NEG_BIG, F32), jnp.zeros((tq, 1), F32),
                            jnp.zeros((tq, LANES), F32), True)

            def body(jj, st, blk=blk):
                return blk(i - 1 - jj, st[0], st[1], st[2], False)

            m, l, acc = lax.fori_loop(0, i, body, (m, l, acc))
            maps.append(acc / l)
        heads.append(maps[0] - lam * maps[1])
    first = lane < DA_V
    o = jnp.where(first, heads[0], heads[1])
    sq = o * o
    ms0 = jnp.sum(jnp.where(first, sq, 0.0), axis=1, keepdims=True) * (1.0 / DA_V)
    ms1 = jnp.sum(jnp.where(first, 0.0, sq), axis=1, keepdims=True) * (1.0 / DA_V)
    ms = jnp.where(first, ms0, ms1)
    o_ref[...] = ((o * lax.rsqrt(ms + RMS_EPS) * subw_ref[...]) * (1.0 - lambda_init)).astype(BF16)


def _da_call(lam_p, q, k, v, subw2, *, batch, seq, tq, lambda_init):
    n = q.shape[0]
    nq = seq // tq
    return pl.pallas_call(
        functools.partial(_da_kernel, tq=tq, lambda_init=lambda_init),
        out_shape=jax.ShapeDtypeStruct((n, DA_W), BF16),
        grid=(batch, DA_W // LANES, nq),
        in_specs=[
            pl.BlockSpec(lam_p.shape, lambda b, p, i: (0, 0)),
            pl.BlockSpec((tq, LANES), lambda b, p, i: (b * nq + i, p)),
            pl.BlockSpec((seq, LANES), lambda b, p, i: (b, p)),
            pl.BlockSpec((seq, LANES), lambda b, p, i: (b, p)),
            pl.BlockSpec((1, LANES), lambda b, p, i: (0, 0)),
        ],
        out_specs=pl.BlockSpec((tq, LANES), lambda b, p, i: (b * nq + i, p)),
        compiler_params=_cparams(("parallel", "parallel", "arbitrary")),
        name="da_attn",
    )(lam_p, q, k, v, subw2)


GDN_CHUNK = 128
CONV_PAD = 8
GDN_PASSES = {"kk": 1, "qk": 1, "sol": 1, "ws": 1, "qs": 1, "av": 1, "kv": 1}


def _gdn_kernel(q_ref, k_ref, v_ref, z_ref, ab_ref, abt_ref, cwq_ref, cwk_ref, cwv_ref, prow_ref, pcol_ref,
                nw_ref, o_ref, state_ref, cq_ref, ck_ref, cv_ref, *, tt):
    hd = pl.program_id(1)
    i = pl.program_id(2)
    c = GDN_CHUNK
    lane = lax.broadcasted_iota(I32, (1, LANES), 1)
    sub8 = lax.broadcasted_iota(I32, (8, 1), 0)
    rows = lax.broadcasted_iota(I32, (c, c), 0)
    cols = lax.broadcasted_iota(I32, (c, c), 1)
    lower = cols <= rows
    strict = cols < rows
    eye = jnp.where(cols == rows, 1.0, 0.0)
    ltri = jnp.where(lower, 1.0, 0.0).astype(BF16)
    utri = jnp.where(rows <= cols, 1.0, 0.0).astype(BF16)

    @pl.when(i == 0)
    def _():
        state_ref[...] = jnp.zeros_like(state_ref)
        for r in (cq_ref, ck_ref, cv_ref):
            r[0:CONV_PAD, :] = jnp.zeros((CONV_PAD, LANES), F32)

    @pl.when(i > 0)
    def _():
        for r in (cq_ref, ck_ref, cv_ref):
            r[0:CONV_PAD, :] = r[tt:tt + CONV_PAD, :]

    def conv_silu(x_ref, buf_ref, w_ref):
        buf_ref[CONV_PAD:CONV_PAD + tt, :] = x_ref[...]
        acc = None
        for j in range(CONV_K):
            off = CONV_PAD - (CONV_K - 1) + j
            term = buf_ref[off:off + tt, :] * w_ref[j:j + 1, :]
            acc = term if acc is None else acc + term
        return _silu(acc)

    qc = conv_silu(q_ref, cq_ref, cwq_ref)
    kc = conv_silu(k_ref, ck_ref, cwk_ref)
    vc = conv_silu(v_ref, cv_ref, cwv_ref)
    qn = qc * lax.rsqrt(jnp.sum(qc * qc, axis=1, keepdims=True) + 1e-6) * (GDN_DK ** -0.5)
    kn = kc * lax.rsqrt(jnp.sum(kc * kc, axis=1, keepdims=True) + 1e-6)

    ab = ab_ref[...]
    g_all = -jnp.exp(prow_ref[0:1, :]) * _softplus(ab + prow_ref[1:2, :])
    g_col = jnp.sum(jnp.where(lane == hd, g_all, 0.0), axis=1, keepdims=True)
    beta_col = jnp.sum(jnp.where(lane == hd + GDN_HEADS, jax.nn.sigmoid(ab), 0.0), axis=1, keepdims=True)
    abt = abt_ref[...]
    g_allt = -jnp.exp(pcol_ref[:, 0:1]) * _softplus(abt + pcol_ref[:, 1:2])
    g_row = jnp.sum(jnp.where(sub8 == hd, g_allt, 0.0), axis=0, keepdims=True)

    for ci in range(tt // c):
        sl = slice(ci * c, (ci + 1) * c)
        q_c, k_c, v_c = qn[sl], kn[sl], vc[sl]
        beta = beta_col[sl]
        gb = jnp.broadcast_to(g_col[sl], (c, LANES))
        g1, g2, g3 = _split3(gb)
        gc = _dot(ltri, g1) + _dot(ltri, g2) + _dot(ltri, g3)
        gr = jnp.broadcast_to(g_row[:, sl], (8, c))
        r1, r2, r3 = _split3(gr)
        gcr = (_dot(r1, utri) + _dot(r2, utri) + _dot(r3, utri))[0:1, :]
        decay = jnp.where(lower, jnp.exp(jnp.where(lower, gc - gcr, 0.0)), 0.0)
        kb = k_c * beta
        kt = k_c.T
        m = jnp.where(strict, _mm(kb, kt, GDN_PASSES["kk"]) * decay, 0.0)
        attn = jnp.where(lower, _mm(q_c, kt, GDN_PASSES["qk"]) * decay, 0.0)
        p = -m
        tinv = eye + p
        for _ in range(int(math.log2(c)) - 2):
            p = _mm(p, p, 1)
            tinv = tinv + _mm(tinv, p, 1)
        resid = eye - _mm(eye + m, tinv, 3)
        tinv = tinv + _mm(tinv, resid, 1)
        eg = jnp.exp(gc)
        rhs = jnp.concatenate([v_c * beta, kb * eg], axis=1)
        sol = _mm(tinv, rhs, GDN_PASSES["sol"])
        u_c = sol[:, :GDN_DV]
        w_c = sol[:, GDN_DV:]
        st = state_ref[...]
        v_new = u_c - _mm(w_c, st, GDN_PASSES["ws"])
        o = _mm(q_c * eg, st, GDN_PASSES["qs"]) + _mm(attn, v_new, GDN_PASSES["av"])
        g_last = gc[c - 1:c, :]
        kd = k_c * jnp.exp(g_last - gc)
        state_ref[...] = st * jnp.exp(g_last) + _mm(kd.T, v_new, GDN_PASSES["kv"])
        on = o * lax.rsqrt(jnp.mean(o * o, axis=1, keepdims=True) + RMS_EPS) * nw_ref[...]
        o_ref[sl, :] = (on * _silu(z_ref[sl, :])).astype(BF16)


def _gdn_call(g4, gab, gabt, conv_w, prow, pcol, nw2, *, batch, seq, tt):
    n = g4.shape[0]
    ns = seq // tt
    tok = lambda off: (lambda b, h, i: (b * ns + i, off + h))
    cw = lambda off: (lambda b, h, i: (0, off + h))
    const = lambda b, h, i: (0, 0)
    return pl.pallas_call(
        functools.partial(_gdn_kernel, tt=tt),
        out_shape=jax.ShapeDtypeStruct((n, GDN_W), BF16),
        grid=(batch, GDN_HEADS, ns),
        in_specs=[
            pl.BlockSpec((tt, LANES), tok(0)),
            pl.BlockSpec((tt, LANES), tok(GDN_HEADS)),
            pl.BlockSpec((tt, LANES), tok(2 * GDN_HEADS)),
            pl.BlockSpec((tt, LANES), tok(3 * GDN_HEADS)),
            pl.BlockSpec((tt, LANES), lambda b, h, i: (b * ns + i, 0)),
            pl.BlockSpec((8, tt), lambda b, h, i: (0, b * ns + i)),
            pl.BlockSpec((CONV_K, LANES), cw(0)),
            pl.BlockSpec((CONV_K, LANES), cw(GDN_HEADS)),
            pl.BlockSpec((CONV_K, LANES), cw(2 * GDN_HEADS)),
            pl.BlockSpec(prow.shape, const),
            pl.BlockSpec(pcol.shape, const),
            pl.BlockSpec((1, LANES), const),
        ],
        out_specs=pl.BlockSpec((tt, LANES), tok(0)),
        scratch_shapes=[
            pltpu.VMEM((GDN_DK, GDN_DV), F32),
            pltpu.VMEM((tt + CONV_PAD, LANES), F32),
            pltpu.VMEM((tt + CONV_PAD, LANES), F32),
            pltpu.VMEM((tt + CONV_PAD, LANES), F32),
        ],
        compiler_params=_cparams(("parallel", "parallel", "arbitrary")),
        name="gdn",
    )(g4, g4, g4, g4, gab, gabt, conv_w, conv_w, conv_w, prow, pcol, nw2)


def _layer_norm(y, g, b):
    mu = jnp.mean(y, axis=1, keepdims=True)
    yc = y - mu
    var = jnp.mean(yc * yc, axis=1, keepdims=True)
    return yc * lax.rsqrt(var + LN_EPS) * g + b


def _outproj_kernel(x_ref, osb_ref, ogdn_ref, oda_ref, w_ref, g1_ref, lng_ref, lnb_ref, sc2_ref, sh2_ref,
                    wr_ref, br_ref, x1_ref, h2_ref, route_ref, *, dn_alpha):
    mix = (_dot(osb_ref[...], w_ref[0:SB_W, :]) + _dot(ogdn_ref[...], w_ref[SB_W:SB_W + GDN_W, :])
           + _dot(oda_ref[...], w_ref[SB_W + GDN_W:, :]))
    y = dn_alpha * x_ref[...] + (1.0 + g1_ref[0]) * mix
    x1 = _layer_norm(y, lng_ref[...], lnb_ref[...])
    x1_ref[...] = x1
    h2 = x1 * (1.0 + sc2_ref[0]) + sh2_ref[0]
    h2_ref[...] = h2
    lane = lax.broadcasted_iota(I32, (1, LANES), 1)
    logits = _dot(h2.astype(BF16), wr_ref[...]) + br_ref[...]
    logits = jnp.where(lane < N_EXPERTS, logits, NEG_BIG)
    vals, idxs = [], []
    for _ in range(TOP_K):
        mx = jnp.max(logits, axis=1, keepdims=True)
        ix = jnp.min(jnp.where(logits == mx, lane, LANES), axis=1, keepdims=True)
        vals.append(mx)
        idxs.append(ix)
        logits = jnp.where(lane == ix, NEG_BIG, logits)
    es = [jnp.exp(v - vals[0]) for v in vals]
    den = es[0] + es[1] + es[2] + es[3]
    route = jnp.zeros(route_ref.shape, F32)
    for kk in range(TOP_K):
        route = jnp.where(lane == kk, es[kk] / den, route)
        route = jnp.where(lane == TOP_K + kk, idxs[kk].astype(F32), route)
    route_ref[...] = route


def _outproj_call(x2, osb, ogdn, oda, w_out, g1, lng, lnb, sc2, sh2, wr, br, *, seq, tm, dn_alpha):
    n, d = x2.shape
    nblk_s = seq // tm
    row = lambda i: (i, 0)
    bat = lambda i: (i // nblk_s, 0, 0)
    const = lambda i: (0, 0)
    return pl.pallas_call(
        functools.partial(_outproj_kernel, dn_alpha=dn_alpha),
        out_shape=[jax.ShapeDtypeStruct((n, d), F32), jax.ShapeDtypeStruct((n, d), F32),
                   jax.ShapeDtypeStruct((n, LANES), F32)],
        grid=(n // tm,),
        in_specs=[
            pl.BlockSpec((tm, d), row),
            pl.BlockSpec((tm, SB_W), row),
            pl.BlockSpec((tm, GDN_W), row),
            pl.BlockSpec((tm, DA_W), row),
            pl.BlockSpec(w_out.shape, const),
            pl.BlockSpec((1, 1, d), bat),
            pl.BlockSpec((1, d), const),
            pl.BlockSpec((1, d), const),
            pl.BlockSpec((1, 1, d), bat),
            pl.BlockSpec((1, 1, d), bat),
            pl.BlockSpec(wr.shape, const),
            pl.BlockSpec((1, LANES), const),
        ],
        out_specs=[pl.BlockSpec((tm, d), row), pl.BlockSpec((tm, d), row), pl.BlockSpec((tm, LANES), row)],
        compiler_params=_cparams(("parallel",)),
        name="outproj_ln_router",
    )(x2, osb, ogdn, oda, w_out, g1, lng, lnb, sc2, sh2, wr, br)


def _moe_kernel(bexp_ref, nvalid_ref, tokc_ref, tokn_ref, h_hbm, wgu_ref, bgu_ref, wd_ref, bd_ref, y_ref, xbuf, sem,
                *, n_tok):
    i = pl.program_id(0)
    nb = pl.num_programs(0)
    slot = i % 2

    def issue(tok_ref, slot_):
        def body(r, carry):
            t = jnp.minimum(tok_ref[0, 0, r], n_tok - 1)
            pltpu.make_async_copy(h_hbm.at[pl.ds(t, 1)], xbuf.at[slot_, pl.ds(r, 1)], sem.at[slot_]).start()
            return carry
        lax.fori_loop(0, MOE_BLOCK, body, 0)

    @pl.when((i == 0) & (nvalid_ref[0] > 0))
    def _():
        issue(tokc_ref, 0)

    @pl.when((i + 1 < nb) & (nvalid_ref[jnp.minimum(i + 1, nb - 1)] > 0))
    def _():
        issue(tokn_ref, 1 - slot)

    @pl.when(nvalid_ref[i] > 0)
    def _():
        pltpu.make_async_copy(h_hbm.at[pl.ds(0, MOE_BLOCK)], xbuf.at[slot], sem.at[slot]).wait()
        xb = xbuf[slot].astype(BF16)
        hgu = _dot(xb, wgu_ref[0]) + bgu_ref[0]
        glu = jnp.minimum(hgu[:, :D_FF], SWIGLU_LIMIT)
        lin = jnp.clip(hgu[:, D_FF:], -SWIGLU_LIMIT, SWIGLU_LIMIT)
        act = glu * jax.nn.sigmoid(SWIGLU_ALPHA * glu) * (lin + 1.0)
        y_ref[...] = _dot(act.astype(BF16), wd_ref[0]) + bd_ref[0]

    @pl.when(nvalid_ref[i] == 0)
    def _():
        y_ref[...] = jnp.zeros_like(y_ref)


def _moe_call(block_expert, nvalid, row_tok3, h2, wgu, bgu, wd, bd):
    n_blocks = block_expert.shape[0]
    n, d = h2.shape
    wsel = lambda i, be, nv: (be[i], 0, 0)
    return pl.pallas_call(
        functools.partial(_moe_kernel, n_tok=n),
        out_shape=jax.ShapeDtypeStruct((n_blocks * MOE_BLOCK, d), F32),
        grid_spec=pltpu.PrefetchScalarGridSpec(
            num_scalar_prefetch=2,
            grid=(n_blocks,),
            in_specs=[
                pl.BlockSpec((1, 1, MOE_BLOCK), lambda i, be, nv: (i, 0, 0), memory_space=pltpu.SMEM),
                pl.BlockSpec((1, 1, MOE_BLOCK), lambda i, be, nv: (jnp.minimum(i + 1, n_blocks - 1), 0, 0),
                             memory_space=pltpu.SMEM),
                pl.BlockSpec(memory_space=pl.ANY),
                pl.BlockSpec((1, d, 2 * D_FF), wsel),
                pl.BlockSpec((1, 1, 2 * D_FF), wsel),
                pl.BlockSpec((1, D_FF, d), wsel),
                pl.BlockSpec((1, 1, d), wsel),
            ],
            out_specs=pl.BlockSpec((MOE_BLOCK, d), lambda i, be, nv: (i, 0)),
            scratch_shapes=[pltpu.VMEM((2, MOE_BLOCK, d), F32), pltpu.SemaphoreType.DMA((2,))],
        ),
        compiler_params=_cparams(("arbitrary",)),
        name="moe_experts",
    )(block_expert, nvalid, row_tok3, row_tok3, h2, wgu, bgu, wd, bd)


def _combine_kernel(posc_ref, posn_ref, y_hbm, x1_ref, route_ref, g2_ref, lng_ref, lnb_ref, o_ref, ybuf, sem,
                    *, tm, dn_alpha):
    i = pl.program_id(0)
    nb = pl.num_programs(0)
    slot = i % 2

    def issue(pos_ref, slot_):
        for kk in range(TOP_K):
            def body(r, carry, kk=kk):
                p = pos_ref[0, 0, kk * tm + r]
                pltpu.make_async_copy(y_hbm.at[pl.ds(p, 1)], ybuf.at[slot_, kk, pl.ds(r, 1)], sem.at[slot_]).start()
                return carry
            lax.fori_loop(0, tm, body, 0)

    @pl.when(i == 0)
    def _():
        issue(posc_ref, 0)

    @pl.when(i + 1 < nb)
    def _():
        issue(posn_ref, 1 - slot)

    for kk in range(TOP_K):
        pltpu.make_async_copy(y_hbm.at[pl.ds(0, tm)], ybuf.at[slot, kk], sem.at[slot]).wait()
    lane = lax.broadcasted_iota(I32, (1, LANES), 1)
    route = route_ref[...]
    ffn = None
    for kk in range(TOP_K):
        gate = jnp.sum(jnp.where(lane == kk, route, 0.0), axis=1, keepdims=True)
        term = gate * ybuf[slot, kk]
        ffn = term if ffn is None else ffn + term
    y = dn_alpha * x1_ref[...] + (1.0 + g2_ref[0]) * ffn
    o_ref[...] = _layer_norm(y, lng_ref[...], lnb_ref[...])


def _combine_call(pos3, y_rows, x1, route, g2, lng, lnb, *, seq, tm, dn_alpha):
    n, d = x1.shape
    nblk_s = seq // tm
    nblk = n // tm
    return pl.pallas_call(
        functools.partial(_combine_kernel, tm=tm, dn_alpha=dn_alpha),
        out_shape=jax.ShapeDtypeStruct((n, d), F32),
        grid=(nblk,),
        in_specs=[
            pl.BlockSpec((1, 1, TOP_K * tm), lambda i: (i, 0, 0), memory_space=pltpu.SMEM),
            pl.BlockSpec((1, 1, TOP_K * tm), lambda i: (jnp.minimum(i + 1, nblk - 1), 0, 0),
                         memory_space=pltpu.SMEM),
            pl.BlockSpec(memory_space=pl.ANY),
            pl.BlockSpec((tm, d), lambda i: (i, 0)),
            pl.BlockSpec((tm, LANES), lambda i: (i, 0)),
            pl.BlockSpec((1, 1, d), lambda i: (i // nblk_s, 0, 0)),
            pl.BlockSpec((1, d), lambda i: (0, 0)),
            pl.BlockSpec((1, d), lambda i: (0, 0)),
        ],
        out_specs=pl.BlockSpec((tm, d), lambda i: (i, 0)),
        scratch_shapes=[pltpu.VMEM((2, TOP_K, tm, d), F32), pltpu.SemaphoreType.DMA((2,))],
        compiler_params=_cparams(("arbitrary",)),
        name="moe_combine_ln",
    )(pos3, pos3, y_rows, x1, route, g2, lng, lnb)


def _routing(route, n):
    e_flat = route[:, TOP_K:2 * TOP_K].astype(I32).reshape(-1)
    tok_flat = jnp.repeat(jnp.arange(n, dtype=I32), TOP_K)
    order = jnp.argsort(e_flat)
    e_sorted = e_flat[order]
    counts = jnp.zeros((N_EXPERTS,), I32).at[e_flat].add(1)
    starts = jnp.cumsum(counts) - counts
    padded = (counts + MOE_BLOCK - 1) // MOE_BLOCK * MOE_BLOCK
    padded_ends = jnp.cumsum(padded)
    padded_starts = padded_ends - padded
    rank = jnp.arange(n * TOP_K, dtype=I32)
    dest = padded_starts[e_sorted] + rank - starts[e_sorted]
    n_rows = -(-(n * TOP_K) // MOE_BLOCK) * MOE_BLOCK + N_EXPERTS * MOE_BLOCK
    n_blocks = n_rows // MOE_BLOCK
    row_tok = jnp.full((n_rows,), n, I32).at[dest].set(tok_flat[order])
    blk_start = jnp.arange(n_blocks, dtype=I32) * MOE_BLOCK
    block_expert = jnp.minimum(jnp.searchsorted(padded_ends, blk_start, side='right'), N_EXPERTS - 1).astype(I32)
    used_end = padded_starts[block_expert] + counts[block_expert]
    nvalid = jnp.clip(used_end - blk_start, 0, MOE_BLOCK).astype(I32)
    nvalid = jnp.where(blk_start < padded_ends[-1], nvalid, 0)
    pos_flat = jnp.zeros((n * TOP_K,), I32).at[order].set(dest)
    return block_expert, nvalid, row_tok.reshape(n_blocks, 1, MOE_BLOCK), pos_flat.reshape(n, TOP_K)


def _rope_tables(seq):
    pos = jnp.arange(seq, dtype=F32)
    inv_freq = ROPE_THETA ** (-jnp.arange(0, ROT_DIM, 2, dtype=F32) / ROT_DIM)
    ang = pos[:, None] * inv_freq[None, :]
    cos, sin = jnp.cos(ang), jnp.sin(ang)
    half = ROT_DIM // 2
    ones = jnp.ones((seq, DA_QK - ROT_DIM), F32)
    zeros = jnp.zeros((seq, DA_QK - ROT_DIM), F32)
    zh = jnp.zeros((seq, half), F32)
    cos_g = jnp.concatenate([cos, cos, ones], axis=1)
    sinm_g = jnp.concatenate([-sin, zh, zeros], axis=1)
    sinp_g = jnp.concatenate([zh, sin, zeros], axis=1)
    rep = LANES // DA_QK
    return jnp.tile(cos_g, (1, rep)), jnp.tile(sinm_g, (1, rep)), jnp.tile(sinp_g, (1, rep))


def kernel(x, c, w_ada, b_ada, w_in, conv_w, gdn_a_log, gdn_dt_bias, gdn_norm_w, da_lambda, da_subln_w, w_out,
           ln1_g, ln1_b, w_router, b_router, w_gu, b_gu, w_down, b_down, ln2_g, ln2_b):
    batch, seq, d = x.shape
    depth = w_ada.shape[0]
    n = batch * seq
    dn_alpha = (2 * depth) ** 0.25
    tm = min(256, seq)
    tq = min(256, seq)
    tt = min(256, seq)
    tc = min(128, seq)

    mod = _ada_call(c, w_ada, b_ada)
    cos_t, sinm_t, sinp_t = _rope_tables(seq)
    x2 = x.reshape(n, d)
    for l in range(depth):
        lambda_init = 0.8 - 0.6 * math.exp(-0.3 * l)
        sh1, sc1, g1, sh2, sc2, g2 = [mod[l, :, j * d:(j + 1) * d].reshape(batch, 1, d) for j in range(6)]
        wl = w_in[l]
        p_sb = 3 * SB_W
        p_g = p_sb + 2 * GDN_KW + 2 * GDN_W
        p_ab = p_g + 2 * GDN_HEADS
        w_main = jnp.concatenate([wl[:, :p_g], wl[:, p_ab:]], axis=1).astype(BF16)
        w_ab = jnp.pad(wl[:, p_g:p_ab], ((0, 0), (0, LANES - 2 * GDN_HEADS))).astype(BF16)
        w_abt = wl[:, p_g:p_ab].T.astype(BF16)
        prow = jnp.zeros((8, LANES), F32).at[0, :GDN_HEADS].set(gdn_a_log[l]).at[1, :GDN_HEADS].set(gdn_dt_bias[l])
        pcol = jnp.zeros((8, LANES), F32).at[:GDN_HEADS, 0].set(gdn_a_log[l]).at[:GDN_HEADS, 1].set(gdn_dt_bias[l])
        nw2 = gdn_norm_w[l].reshape(1, GDN_DV)
        subw2 = jnp.tile(da_subln_w[l], LANES // DA_V).reshape(1, LANES)
        wr = jnp.pad(w_router[l], ((0, 0), (0, LANES - N_EXPERTS))).astype(BF16)
        br = jnp.pad(b_router[l], (0, LANES - N_EXPERTS)).reshape(1, LANES)
        wgu = jnp.concatenate([w_gu[l][..., 0::2], w_gu[l][..., 1::2]], axis=-1).astype(BF16)
        bgu = jnp.concatenate([b_gu[l][..., 0::2], b_gu[l][..., 1::2]], axis=-1).reshape(N_EXPERTS, 1, 2 * D_FF)
        wd = w_down[l].astype(BF16)
        bd = b_down[l].reshape(N_EXPERTS, 1, d)

        sbq, sbk, sbv, g4, daq, dak, dav, gab, gabt = _inproj_call(
            x2, sc1, sh1, w_main, w_ab, w_abt, cos_t, sinm_t, sinp_t, seq=seq, tm=tm)
        o_sb = _sb_call(sbq, sbk, sbv, batch=batch, seq=seq, tq=tq)
        o_gdn = _gdn_call(g4, gab, gabt, conv_w[l], prow, pcol, nw2, batch=batch, seq=seq, tt=tt)
        o_da = _da_call(da_lambda[l], daq, dak, dav, subw2, batch=batch, seq=seq, tq=tq, lambda_init=lambda_init)
        x1, h2, route = _outproj_call(
            x2, o_sb, o_gdn, o_da, w_out[l].astype(BF16), g1, ln1_g[l].reshape(1, d), ln1_b[l].reshape(1, d),
            sc2, sh2, wr, br, seq=seq, tm=tm, dn_alpha=dn_alpha)
        block_expert, nvalid, row_tok3, pos = _routing(route, n)
        y_rows = _moe_call(block_expert, nvalid, row_tok3, h2, wgu, bgu, wd, bd)
        pos3 = pos.reshape(n // tc, tc, TOP_K).transpose(0, 2, 1).reshape(n // tc, 1, TOP_K * tc)
        x2 = _combine_call(pos3, y_rows, x1, route, g2, ln2_g[l].reshape(1, d), ln2_b[l].reshape(1, d),
                           seq=seq, tm=tc, dn_alpha=dn_alpha)
    return x2.reshape(batch, seq, d)
```

```python
import functools
import math

import jax
import jax.numpy as jnp
from jax import lax
from jax.experimental import pallas as pl
from jax.experimental.pallas import tpu as pltpu

F32 = jnp.float32
BF16 = jnp.bfloat16
I32 = jnp.int32

LANES = 128
VMEM_LIMIT = 56 * 1024 * 1024

D_MODEL = 1024
SB_HEADS = 4
SB_DIM = 64
GDN_HEADS = 4
GDN_DK = 128
GDN_DV = 128
DA_HEADS = 4
DA_V = 64
DA_QK = 32
ROT_DIM = DA_QK // 4
ROPE_THETA = 500000.0
SB_W = SB_HEADS * SB_DIM
GDN_KW = GDN_HEADS * GDN_DK
GDN_W = GDN_HEADS * GDN_DV
DA_W = DA_HEADS * DA_V
CONV_K = 4
N_EXPERTS = 32
TOP_K = 4
D_FF = D_MODEL
SWIGLU_ALPHA = 1.702
SWIGLU_LIMIT = 7.0
MOE_BLOCK = 256
LN_EPS = 1e-5
RMS_EPS = 1e-6
NEG_BIG = -1e30
LOG2E = math.log2(math.e)
DMA_UNROLL = 8

C_SB = 0
C_G4 = 3 * SB_W
C_DQK = C_G4 + 4 * GDN_W
C_DV = C_DQK + 2 * DA_W
W_MAIN = C_DV + DA_W


def _dot(a, b):
    return jnp.dot(a, b, preferred_element_type=F32)


def _dot_nt(a, b):
    return lax.dot_general(a, b, (((1,), (1,)), ((), ())), preferred_element_type=F32)


def _dot_tn(a, b):
    return lax.dot_general(a, b, (((0,), (0,)), ((), ())), preferred_element_type=F32)


def _split3(x):
    h1 = x.astype(BF16)
    r1 = x - h1.astype(F32)
    h2 = r1.astype(BF16)
    h3 = (r1 - h2.astype(F32)).astype(BF16)
    return h1, h2, h3


def _mm(a, b, passes):
    ah = a.astype(BF16)
    bh = b.astype(BF16)
    if passes == 1:
        return _dot(ah, bh)
    al = (a - ah.astype(F32)).astype(BF16)
    bl = (b - bh.astype(F32)).astype(BF16)
    return _dot(ah, bh) + (_dot(al, bh) + _dot(ah, bl))


def _silu(x):
    return x * jax.nn.sigmoid(x)


def _softplus(x):
    return jnp.maximum(x, 0.0) + jnp.log(1.0 + jnp.exp(-jnp.abs(x)))


def _cparams(sem):
    return pltpu.CompilerParams(dimension_semantics=sem, vmem_limit_bytes=VMEM_LIMIT)


def _ada_kernel(c_ref, w_ref, b_ref, o_ref):
    cond = _silu(c_ref[...])
    w = w_ref[0]
    c1, c2, c3 = _split3(cond)
    w1, w2, w3 = _split3(w)
    acc = _dot(c1, w1) + (_dot(c1, w2) + _dot(c2, w1)) + (_dot(c2, w2) + _dot(c1, w3) + _dot(c3, w1))
    o_ref[0] = acc + b_ref[0]


def _ada_call(c, w_ada, b_ada):
    depth, d, d6 = w_ada.shape
    b = c.shape[0]
    nj = d6 // d
    return pl.pallas_call(
        _ada_kernel,
        out_shape=jax.ShapeDtypeStruct((depth, b, d6), F32),
        grid=(depth, nj),
        in_specs=[
            pl.BlockSpec((b, d), lambda l, j: (0, 0)),
            pl.BlockSpec((1, d, d), lambda l, j: (l, 0, j)),
            pl.BlockSpec((1, 1, d), lambda l, j: (l, 0, j)),
        ],
        out_specs=pl.BlockSpec((1, b, d), lambda l, j: (l, 0, j)),
        compiler_params=_cparams(("parallel", "parallel")),
        name="ada_mod",
    )(c, w_ada, b_ada.reshape(depth, 1, d6))


def _inproj_kernel(x_ref, sc_ref, sh_ref, w_ref, wab_ref, wabt_ref, cos_ref, sinm_ref, sinp_ref,
                   sbq_ref, sbk_ref, sbv_ref, g4_ref, daq_ref, dak_ref, dav_ref, gab_ref, gabt_ref):
    h = (x_ref[...] * (1.0 + sc_ref[0]) + sh_ref[0]).astype(BF16)

    def seg(a, n):
        return _dot(h, w_ref[:, a:a + n])

    sbq_ref[...] = (seg(C_SB, SB_W) * (SB_DIM ** -0.5 * LOG2E)).astype(BF16)
    sbk_ref[...] = seg(C_SB + SB_W, SB_W).astype(BF16)
    sbv_ref[...] = seg(C_SB + 2 * SB_W, SB_W).astype(BF16)
    for j in range(4):
        g4_ref[:, j * GDN_W:(j + 1) * GDN_W] = seg(C_G4 + j * GDN_W, GDN_W)
    cos = cos_ref[...]
    sinm = sinm_ref[...]
    sinp = sinp_ref[...]
    for ref, base, scale in ((daq_ref, C_DQK, DA_QK ** -0.5 * LOG2E), (dak_ref, C_DQK + DA_W, 1.0)):
        for j in range(DA_W // LANES):
            a = seg(base + j * LANES, LANES)
            r = a * cos + pltpu.roll(a, LANES - ROT_DIM // 2, 1) * sinm + pltpu.roll(a, ROT_DIM // 2, 1) * sinp
            ref[:, j * LANES:(j + 1) * LANES] = (r * scale).astype(BF16)
    dav_ref[...] = seg(C_DV, DA_W).astype(BF16)
    gab_ref[...] = _dot(h, wab_ref[...])
    gabt_ref[...] = _dot_nt(wabt_ref[...], h)


def _inproj_call(x2, sc, sh, w_main, w_ab, w_abt, cos_t, sinm_t, sinp_t, *, seq, tm):
    n, d = x2.shape
    nblk_s = seq // tm
    row = lambda i: (i, 0)
    bat = lambda i: (i // nblk_s, 0, 0)
    pos = lambda i: (i % nblk_s, 0)
    const = lambda i: (0, 0)
    outs = [
        (SB_W, BF16), (SB_W, BF16), (SB_W, BF16), (4 * GDN_W, F32),
        (DA_W, BF16), (DA_W, BF16), (DA_W, BF16), (LANES, F32),
    ]
    out_shape = [jax.ShapeDtypeStruct((n, w), dt) for w, dt in outs] + [jax.ShapeDtypeStruct((8, n), F32)]
    out_specs = [pl.BlockSpec((tm, w), row) for w, _ in outs] + [pl.BlockSpec((8, tm), lambda i: (0, i))]
    return pl.pallas_call(
        _inproj_kernel,
        out_shape=out_shape,
        grid=(n // tm,),
        in_specs=[
            pl.BlockSpec((tm, d), row),
            pl.BlockSpec((1, 1, d), bat),
            pl.BlockSpec((1, 1, d), bat),
            pl.BlockSpec(w_main.shape, const),
            pl.BlockSpec(w_ab.shape, const),
            pl.BlockSpec(w_abt.shape, const),
            pl.BlockSpec((tm, LANES), pos),
            pl.BlockSpec((tm, LANES), pos),
            pl.BlockSpec((tm, LANES), pos),
        ],
        out_specs=out_specs,
        compiler_params=_cparams(("parallel",)),
        name="inproj",
    )(x2, sc, sh, w_main, w_ab, w_abt, cos_t, sinm_t, sinp_t)


def _sb_kernel(q_ref, k_ref, v_ref, o_ref, *, tq):
    i = pl.program_id(2)
    q = q_ref[...]
    lane = lax.broadcasted_iota(I32, (1, LANES), 1)
    rows = lax.broadcasted_iota(I32, (tq, tq), 0)
    cols = lax.broadcasted_iota(I32, (tq, tq), 1)
    tri = cols < rows
    u = jnp.where(tri, 1.0, 0.0).astype(BF16)
    qhs = [jnp.where((lane >= SB_DIM * hh) & (lane < SB_DIM * (hh + 1)), q, jnp.zeros_like(q)) for hh in range(2)]

    def scores(j):
        kb = k_ref[pl.ds(pl.multiple_of(j * tq, tq), tq), :]
        return tuple(_dot_nt(qh, kb) for qh in qhs)

    def blk(j, zs, state, masked):
        vb = v_ref[pl.ds(pl.multiple_of(j * tq, tq), tq), :]
        new = []
        for hh in range(2):
            carry, acc = state[hh]
            z = zs[hh]
            sp = jnp.maximum(z, 0.0) + jnp.log2(1.0 + jnp.exp2(-jnp.abs(z)))
            lk = -sp
            if masked:
                lk = jnp.where(tri, lk, 0.0)
            hi = lk.astype(BF16)
            lo = (lk - hi.astype(F32)).astype(BF16)
            between = _dot(hi, u) + _dot(lo, u)
            w = jnp.exp2(z - sp + between + carry)
            if masked:
                w = jnp.where(tri, w, 0.0)
            acc = acc + _dot(w.astype(BF16), vb)
            carry = carry + jnp.sum(lk, axis=1, keepdims=True)
            new.append((carry, acc))
        return tuple(new)

    zero = (jnp.zeros((tq, 1), F32), jnp.zeros((tq, LANES), F32))
    z_next = scores(jnp.maximum(i - 1, 0))
    state = blk(i, scores(i), (zero, zero), True)

    def body(jj, st):
        zs, state = st
        j = i - 1 - jj
        z_ahead = scores(jnp.maximum(j - 1, 0))
        return z_ahead, blk(j, zs, state, False)

    _, state = lax.fori_loop(0, i, body, (z_next, state))
    o_ref[...] = jnp.where(lane < SB_DIM, state[0][1], state[1][1]).astype(BF16)


def _sb_call(q, k, v, *, batch, seq, tq):
    n = q.shape[0]
    nq = seq // tq
    return pl.pallas_call(
        functools.partial(_sb_kernel, tq=tq),
        out_shape=jax.ShapeDtypeStruct((n, SB_W), BF16),
        grid=(batch, SB_W // LANES, nq),
        in_specs=[
            pl.BlockSpec((tq, LANES), lambda b, p, i: (b * nq + i, p)),
            pl.BlockSpec((seq, LANES), lambda b, p, i: (b, p)),
            pl.BlockSpec((seq, LANES), lambda b, p, i: (b, p)),
        ],
        out_specs=pl.BlockSpec((tq, LANES), lambda b, p, i: (b * nq + i, p)),
        compiler_params=_cparams(("parallel", "parallel", "arbitrary")),
        name="sb_attn",
    )(q, k, v)


def _da_kernel(lam_ref, q_ref, k_ref, v_ref, subw_ref, o_ref, *, tq, lambda_init):
    i = pl.program_id(2)
    lp = lam_ref[...]
    lam = (jnp.exp(jnp.sum(lp[0:1] * lp[1:2], axis=1, keepdims=True))
           - jnp.exp(jnp.sum(lp[2:3] * lp[3:4], axis=1, keepdims=True)) + lambda_init)
    q = q_ref[...]
    lane = lax.broadcasted_iota(I32, (1, LANES), 1)
    rows = lax.broadcasted_iota(I32, (tq, tq), 0)
    cols = lax.broadcasted_iota(I32, (tq, tq), 1)
    tri = cols <= rows
    first = lane < DA_V
    n_maps = LANES // DA_QK
    qms = [jnp.where((lane >= DA_QK * g) & (lane < DA_QK * (g + 1)), q, jnp.zeros_like(q)) for g in range(n_maps)]

    def scores(j):
        kb = k_ref[pl.ds(pl.multiple_of(j * tq, tq), tq), :]
        return tuple(_dot_nt(qm, kb) for qm in qms)

    def blk(j, ss, state, masked):
        vb = v_ref[pl.ds(pl.multiple_of(j * tq, tq), tq), :]
        vh = (jnp.where(first, vb, jnp.ones_like(vb)), jnp.where(first, jnp.ones_like(vb), vb))
        new = []
        for g in range(n_maps):
            m, acc = state[g]
            s = ss[g]
            if masked:
                s = jnp.where(tri, s, NEG_BIG)
            m_new = jnp.maximum(m, jnp.max(s, axis=1, keepdims=True))
            alpha = jnp.exp2(m - m_new)
            p = jnp.exp2(s - m_new)
            acc = alpha * acc + _dot(p.astype(BF16), vh[g // 2])
            new.append((m_new, acc))
        return tuple(new)

    init = (jnp.full((tq, 1), NEG_BIG, F32), jnp.zeros((tq, LANES), F32))
    s_next = scores(jnp.maximum(i - 1, 0))
    state = blk(i, scores(i), (init,) * n_maps, True)

    def body(jj, st):
        ss, state = st
        j = i - 1 - jj
        s_ahead = scores(jnp.maximum(j - 1, 0))
        return s_ahead, blk(j, ss, state, False)

    _, state = lax.fori_loop(0, i, body, (s_next, state))
    maps = [acc / pltpu.roll(acc, DA_V, 1) for _, acc in state]
    o = jnp.where(first, maps[0] - lam * maps[1], maps[2] - lam * maps[3])
    sq = o * o
    ms0 = jnp.sum(jnp.where(first, sq, 0.0), axis=1, keepdims=True) * (1.0 / DA_V)
    ms1 = jnp.sum(jnp.where(first, 0.0, sq), axis=1, keepdims=True) * (1.0 / DA_V)
    ms = jnp.where(first, ms0, ms1)
    o_ref[...] = ((o * lax.rsqrt(ms + RMS_EPS) * subw_ref[...]) * (1.0 - lambda_init)).astype(BF16)


def _da_call(lam_p, q, k, v, subw2, *, batch, seq, tq, lambda_init):
    n = q.shape[0]
    nq = seq // tq
    return pl.pallas_call(
        functools.partial(_da_kernel, tq=tq, lambda_init=lambda_init),
        out_shape=jax.ShapeDtypeStruct((n, DA_W), BF16),
        grid=(batch, DA_W // LANES, nq),
        in_specs=[
            pl.BlockSpec(lam_p.shape, lambda b, p, i: (0, 0)),
            pl.BlockSpec((tq, LANES), lambda b, p, i: (b * nq + i, p)),
            pl.BlockSpec((seq, LANES), lambda b, p, i: (b, p)),
            pl.BlockSpec((seq, LANES), lambda b, p, i: (b, p)),
            pl.BlockSpec((1, LANES), lambda b, p, i: (0, 0)),
        ],
        out_specs=pl.BlockSpec((tq, LANES), lambda b, p, i: (b * nq + i, p)),
        compiler_params=_cparams(("parallel", "parallel", "arbitrary")),
        name="da_attn",
    )(lam_p, q, k, v, subw2)


GDN_CHUNK = 128
CONV_PAD = 8
GDN_TT = 512


def _gdn_kernel(q_ref, k_ref, v_ref, z_ref, ab_ref, abt_ref, cwq_ref, cwk_ref, cwv_ref, prow_ref, pcol_ref,
                nw_ref, o_ref, state_ref, cq_ref, ck_ref, cv_ref, *, tt):
    hd = pl.program_id(1)
    i = pl.program_id(2)
    c = GDN_CHUNK
    lane = lax.broadcasted_iota(I32, (1, LANES), 1)
    sub8 = lax.broadcasted_iota(I32, (8, 1), 0)
    rows = lax.broadcasted_iota(I32, (c, c), 0)
    cols = lax.broadcasted_iota(I32, (c, c), 1)
    lower = cols <= rows
    strict = cols < rows
    eye = jnp.where(cols == rows, 1.0, 0.0)
    ltri = jnp.where(lower, 1.0, 0.0).astype(BF16)
    utri = jnp.where(rows <= cols, 1.0, 0.0).astype(BF16)

    @pl.when(i == 0)
    def _():
        state_ref[...] = jnp.zeros_like(state_ref)
        for r in (cq_ref, ck_ref, cv_ref):
            r[0:CONV_PAD, :] = jnp.zeros((CONV_PAD, LANES), F32)

    @pl.when(i > 0)
    def _():
        for r in (cq_ref, ck_ref, cv_ref):
            r[0:CONV_PAD, :] = r[tt:tt + CONV_PAD, :]

    def conv_silu(x_ref, buf_ref, w_ref):
        buf_ref[CONV_PAD:CONV_PAD + tt, :] = x_ref[...]
        acc = None
        for j in range(CONV_K):
            off = CONV_PAD - (CONV_K - 1) + j
            term = buf_ref[off:off + tt, :] * w_ref[j:j + 1, :]
            acc = term if acc is None else acc + term
        return _silu(acc)

    qc = conv_silu(q_ref, cq_ref, cwq_ref)
    kc = conv_silu(k_ref, ck_ref, cwk_ref)
    vc = conv_silu(v_ref, cv_ref, cwv_ref)
    qn = qc * lax.rsqrt(jnp.sum(qc * qc, axis=1, keepdims=True) + 1e-6) * (GDN_DK ** -0.5)
    kn = kc * lax.rsqrt(jnp.sum(kc * kc, axis=1, keepdims=True) + 1e-6)

    ab = ab_ref[...]
    g_all = -jnp.exp(prow_ref[0:1, :]) * _softplus(ab + prow_ref[1:2, :])
    g_col = jnp.sum(jnp.where(lane == hd, g_all, 0.0), axis=1, keepdims=True)
    beta_col = jnp.sum(jnp.where(lane == hd + GDN_HEADS, jax.nn.sigmoid(ab), 0.0), axis=1, keepdims=True)
    abt = abt_ref[...]
    g_allt = -jnp.exp(pcol_ref[:, 0:1]) * _softplus(abt + pcol_ref[:, 1:2])
    g_row = jnp.sum(jnp.where(sub8 == hd, g_allt, 0.0), axis=0, keepdims=True)

    nc = tt // c
    sls = [slice(ci * c, (ci + 1) * c) for ci in range(nc)]
    gcs, decays, kbs, ms, attns = [], [], [], [], []
    for sl in sls:
        g1, g2, g3 = _split3(jnp.broadcast_to(g_col[sl], (c, LANES)))
        gc = _dot(ltri, g1) + _dot(ltri, g2) + _dot(ltri, g3)
        r1, r2, r3 = _split3(jnp.broadcast_to(g_row[:, sl], (8, c)))
        gcr = (_dot(r1, utri) + _dot(r2, utri) + _dot(r3, utri))[0:1, :]
        gcs.append(gc)
        decays.append(jnp.where(lower, jnp.exp(jnp.where(lower, gc - gcr, 0.0)), 0.0))
    for sl, decay in zip(sls, decays):
        kb = kn[sl] * beta_col[sl]
        kt = kn[sl].T
        kbs.append(kb)
        ms.append(jnp.where(strict, _mm(kb, kt, 1) * decay, 0.0))
        attns.append(jnp.where(lower, _mm(qn[sl], kt, 1) * decay, 0.0))
    ps = [-m for m in ms]
    tinvs = [eye + p for p in ps]
    for _ in range(int(math.log2(c)) - 2):
        ps = [_mm(p, p, 1) for p in ps]
        tinvs = [t + _mm(t, p, 1) for t, p in zip(tinvs, ps)]
    resids = [eye - _mm(eye + m, t, 3) for m, t in zip(ms, tinvs)]
    tinvs = [t + _mm(t, r, 1) for t, r in zip(tinvs, resids)]
    egs = [jnp.exp(gc) for gc in gcs]
    sols = [_mm(t, jnp.concatenate([vc[sl] * beta_col[sl], kb * eg], axis=1), 1)
            for t, sl, kb, eg in zip(tinvs, sls, kbs, egs)]
    for ci, sl in enumerate(sls):
        gc, eg = gcs[ci], egs[ci]
        u_c = sols[ci][:, :GDN_DV]
        w_c = sols[ci][:, GDN_DV:]
        st = state_ref[...]
        v_new = u_c - _mm(w_c, st, 1)
        o = _mm(qn[sl] * eg, st, 1) + _mm(attns[ci], v_new, 1)
        g_last = gc[c - 1:c, :]
        kd = kn[sl] * jnp.exp(g_last - gc)
        state_ref[...] = st * jnp.exp(g_last) + _mm(kd.T, v_new, 1)
        on = o * lax.rsqrt(jnp.mean(o * o, axis=1, keepdims=True) + RMS_EPS) * nw_ref[...]
        o_ref[sl, :] = (on * _silu(z_ref[sl, :])).astype(BF16)


def _gdn_call(g4, gab, gabt, conv_w, prow, pcol, nw2, *, batch, seq, tt):
    n = g4.shape[0]
    ns = seq // tt
    tok = lambda off: (lambda b, h, i: (b * ns + i, off + h))
    cw = lambda off: (lambda b, h, i: (0, off + h))
    const = lambda b, h, i: (0, 0)
    return pl.pallas_call(
        functools.partial(_gdn_kernel, tt=tt),
        out_shape=jax.ShapeDtypeStruct((n, GDN_W), BF16),
        grid=(batch, GDN_HEADS, ns),
        in_specs=[
            pl.BlockSpec((tt, LANES), tok(0)),
            pl.BlockSpec((tt, LANES), tok(GDN_HEADS)),
            pl.BlockSpec((tt, LANES), tok(2 * GDN_HEADS)),
            pl.BlockSpec((tt, LANES), tok(3 * GDN_HEADS)),
            pl.BlockSpec((tt, LANES), lambda b, h, i: (b * ns + i, 0)),
            pl.BlockSpec((8, tt), lambda b, h, i: (0, b * ns + i)),
            pl.BlockSpec((CONV_K, LANES), cw(0)),
            pl.BlockSpec((CONV_K, LANES), cw(GDN_HEADS)),
            pl.BlockSpec((CONV_K, LANES), cw(2 * GDN_HEADS)),
            pl.BlockSpec(prow.shape, const),
            pl.BlockSpec(pcol.shape, const),
            pl.BlockSpec((1, LANES), const),
        ],
        out_specs=pl.BlockSpec((tt, LANES), tok(0)),
        scratch_shapes=[
            pltpu.VMEM((GDN_DK, GDN_DV), F32),
            pltpu.VMEM((tt + CONV_PAD, LANES), F32),
            pltpu.VMEM((tt + CONV_PAD, LANES), F32),
            pltpu.VMEM((tt + CONV_PAD, LANES), F32),
        ],
        compiler_params=_cparams(("parallel", "parallel", "arbitrary")),
        name="gdn",
    )(g4, g4, g4, g4, gab, gabt, conv_w, conv_w, conv_w, prow, pcol, nw2)


def _layer_norm(y, g, b):
    mu = jnp.mean(y, axis=1, keepdims=True)
    yc = y - mu
    var = jnp.mean(yc * yc, axis=1, keepdims=True)
    return yc * lax.rsqrt(var + LN_EPS) * g + b


def _outproj_kernel(x_ref, osb_ref, ogdn_ref, oda_ref, w_ref, g1_ref, lng_ref, lnb_ref, sc2_ref, sh2_ref,
                    wr_ref, br_ref, x1_ref, h2_ref, route_ref, *, dn_alpha):
    mix = (_dot(osb_ref[...], w_ref[0:SB_W, :]) + _dot(ogdn_ref[...], w_ref[SB_W:SB_W + GDN_W, :])
           + _dot(oda_ref[...], w_ref[SB_W + GDN_W:, :]))
    y = dn_alpha * x_ref[...] + (1.0 + g1_ref[0]) * mix
    x1 = _layer_norm(y, lng_ref[...], lnb_ref[...])
    x1_ref[...] = x1
    h2 = x1 * (1.0 + sc2_ref[0]) + sh2_ref[0]
    h2_ref[...] = h2
    lane = lax.broadcasted_iota(I32, (1, LANES), 1)
    logits = _dot(h2.astype(BF16), wr_ref[...]) + br_ref[...]
    logits = jnp.where(lane < N_EXPERTS, logits, NEG_BIG)
    vals, idxs = [], []
    for _ in range(TOP_K):
        mx = jnp.max(logits, axis=1, keepdims=True)
        ix = jnp.min(jnp.where(logits == mx, lane, LANES), axis=1, keepdims=True)
        vals.append(mx)
        idxs.append(ix)
        logits = jnp.where(lane == ix, NEG_BIG, logits)
    es = [jnp.exp(v - vals[0]) for v in vals]
    den = es[0] + es[1] + es[2] + es[3]
    route = jnp.zeros(route_ref.shape, F32)
    for kk in range(TOP_K):
        route = jnp.where(lane == kk, es[kk] / den, route)
        route = jnp.where(lane == TOP_K + kk, idxs[kk].astype(F32), route)
    route_ref[...] = route


def _outproj_call(x2, osb, ogdn, oda, w_out, g1, lng, lnb, sc2, sh2, wr, br, *, seq, tm, dn_alpha):
    n, d = x2.shape
    nblk_s = seq // tm
    row = lambda i: (i, 0)
    bat = lambda i: (i // nblk_s, 0, 0)
    const = lambda i: (0, 0)
    return pl.pallas_call(
        functools.partial(_outproj_kernel, dn_alpha=dn_alpha),
        out_shape=[jax.ShapeDtypeStruct((n, d), F32), jax.ShapeDtypeStruct((n, d), F32),
                   jax.ShapeDtypeStruct((n, LANES), F32)],
        grid=(n // tm,),
        in_specs=[
            pl.BlockSpec((tm, d), row),
            pl.BlockSpec((tm, SB_W), row),
            pl.BlockSpec((tm, GDN_W), row),
            pl.BlockSpec((tm, DA_W), row),
            pl.BlockSpec(w_out.shape, const),
            pl.BlockSpec((1, 1, d), bat),
            pl.BlockSpec((1, d), const),
            pl.BlockSpec((1, d), const),
            pl.BlockSpec((1, 1, d), bat),
            pl.BlockSpec((1, 1, d), bat),
            pl.BlockSpec(wr.shape, const),
            pl.BlockSpec((1, LANES), const),
        ],
        out_specs=[pl.BlockSpec((tm, d), row), pl.BlockSpec((tm, d), row), pl.BlockSpec((tm, LANES), row)],
        compiler_params=_cparams(("parallel",)),
        name="outproj_ln_router",
    )(x2, osb, ogdn, oda, w_out, g1, lng, lnb, sc2, sh2, wr, br)


PREP_ROWS = 256
PREP_COLS = 512


def _wprep_kernel(wgu_ref, wd_ref, ogu_ref, od_ref):
    rows = lax.broadcasted_iota(I32, (PREP_COLS, PREP_COLS // 2), 0)
    cols = lax.broadcasted_iota(I32, (PREP_COLS, PREP_COLS // 2), 1)
    sel_even = jnp.where(rows == 2 * cols, 1.0, 0.0).astype(BF16)
    sel_odd = jnp.where(rows == 2 * cols + 1, 1.0, 0.0).astype(BF16)
    half = PREP_COLS // 2
    for j in range(2 * D_FF // PREP_COLS):
        w = wgu_ref[0, 0, :, j * PREP_COLS:(j + 1) * PREP_COLS].astype(BF16)
        ogu_ref[0, 0, :, j * half:(j + 1) * half] = _dot(w, sel_even).astype(BF16)
        ogu_ref[0, 0, :, D_FF + j * half:D_FF + (j + 1) * half] = _dot(w, sel_odd).astype(BF16)
    od_ref[...] = wd_ref[...].astype(BF16)


def _wprep_call(w_gu, w_down):
    depth, ne, d, f2 = w_gu.shape
    nr = d // PREP_ROWS
    return pl.pallas_call(
        _wprep_kernel,
        out_shape=[jax.ShapeDtypeStruct(w_gu.shape, BF16), jax.ShapeDtypeStruct(w_down.shape, BF16)],
        grid=(depth, ne, nr),
        in_specs=[
            pl.BlockSpec((1, 1, PREP_ROWS, f2), lambda l, e, r: (l, e, r, 0)),
            pl.BlockSpec((1, 1, PREP_ROWS, d), lambda l, e, r: (l, e, r, 0)),
        ],
        out_specs=[
            pl.BlockSpec((1, 1, PREP_ROWS, f2), lambda l, e, r: (l, e, r, 0)),
            pl.BlockSpec((1, 1, PREP_ROWS, d), lambda l, e, r: (l, e, r, 0)),
        ],
        compiler_params=_cparams(("parallel", "parallel", "parallel")),
        name="expert_weight_prep",
    )(w_gu, w_down)


def _moe_kernel(bexp_ref, nvalid_ref, tokc_ref, tokn_ref, h_hbm, wgu_ref, bgu_ref, wd_ref, bd_ref, y_ref, xbuf, sem,
                *, n_tok):
    i = pl.program_id(0)
    nb = pl.num_programs(0)
    slot = i % 2

    def issue(tok_ref, slot_):
        def body(r, carry):
            t = jnp.minimum(tok_ref[0, 0, r], n_tok - 1)
            pltpu.make_async_copy(h_hbm.at[pl.ds(t, 1)], xbuf.at[slot_, pl.ds(r, 1)], sem.at[slot_]).start()
            return carry
        lax.fori_loop(0, MOE_BLOCK, body, 0, unroll=DMA_UNROLL)

    @pl.when((i == 0) & (nvalid_ref[0] > 0))
    def _():
        issue(tokc_ref, 0)

    @pl.when((i + 1 < nb) & (nvalid_ref[jnp.minimum(i + 1, nb - 1)] > 0))
    def _():
        issue(tokn_ref, 1 - slot)

    @pl.when(nvalid_ref[i] > 0)
    def _():
        pltpu.make_async_copy(h_hbm.at[pl.ds(0, MOE_BLOCK)], xbuf.at[slot], sem.at[slot]).wait()
        xb = xbuf[slot].astype(BF16)
        hgu = _dot(xb, wgu_ref[0, 0]) + bgu_ref[0]
        glu = jnp.minimum(hgu[:, :D_FF], SWIGLU_LIMIT)
        lin = jnp.clip(hgu[:, D_FF:], -SWIGLU_LIMIT, SWIGLU_LIMIT)
        act = glu * jax.nn.sigmoid(SWIGLU_ALPHA * glu) * (lin + 1.0)
        y_ref[...] = _dot(act.astype(BF16), wd_ref[0, 0]) + bd_ref[0]

    @pl.when(nvalid_ref[i] == 0)
    def _():
        y_ref[...] = jnp.zeros_like(y_ref)


def _moe_call(block_expert, nvalid, row_tok3, h2, wgu_all, bgu, wd_all, bd, *, layer):
    n_blocks = block_expert.shape[0]
    n, d = h2.shape
    wsel = lambda i, be, nv: (layer, be[i], 0, 0)
    bsel = lambda i, be, nv: (be[i], 0, 0)
    return pl.pallas_call(
        functools.partial(_moe_kernel, n_tok=n),
        out_shape=jax.ShapeDtypeStruct((n_blocks * MOE_BLOCK, d), F32),
        grid_spec=pltpu.PrefetchScalarGridSpec(
            num_scalar_prefetch=2,
            grid=(n_blocks,),
            in_specs=[
                pl.BlockSpec((1, 1, MOE_BLOCK), lambda i, be, nv: (i, 0, 0), memory_space=pltpu.SMEM),
                pl.BlockSpec((1, 1, MOE_BLOCK), lambda i, be, nv: (jnp.minimum(i + 1, n_blocks - 1), 0, 0),
                             memory_space=pltpu.SMEM),
                pl.BlockSpec(memory_space=pl.ANY),
                pl.BlockSpec((1, 1, d, 2 * D_FF), wsel),
                pl.BlockSpec((1, 1, 2 * D_FF), bsel),
                pl.BlockSpec((1, 1, D_FF, d), wsel),
                pl.BlockSpec((1, 1, d), bsel),
            ],
            out_specs=pl.BlockSpec((MOE_BLOCK, d), lambda i, be, nv: (i, 0)),
            scratch_shapes=[pltpu.VMEM((2, MOE_BLOCK, d), F32), pltpu.SemaphoreType.DMA((2,))],
        ),
        compiler_params=_cparams(("arbitrary",)),
        name="moe_experts",
    )(block_expert, nvalid, row_tok3, row_tok3, h2, wgu_all, bgu, wd_all, bd)


def _combine_kernel(posc_ref, posn_ref, y_hbm, x1_ref, route_ref, g2_ref, lng_ref, lnb_ref, o_ref, ybuf, sem,
                    *, tm, dn_alpha):
    i = pl.program_id(0)
    nb = pl.num_programs(0)
    slot = i % 2

    def issue(pos_ref, slot_):
        for kk in range(TOP_K):
            def body(r, carry, kk=kk):
                p = pos_ref[0, 0, kk * tm + r]
                pltpu.make_async_copy(y_hbm.at[pl.ds(p, 1)], ybuf.at[slot_, kk, pl.ds(r, 1)], sem.at[slot_]).start()
                return carry
            lax.fori_loop(0, tm, body, 0, unroll=DMA_UNROLL)

    @pl.when(i == 0)
    def _():
        issue(posc_ref, 0)

    @pl.when(i + 1 < nb)
    def _():
        issue(posn_ref, 1 - slot)

    for kk in range(TOP_K):
        pltpu.make_async_copy(y_hbm.at[pl.ds(0, tm)], ybuf.at[slot, kk], sem.at[slot]).wait()
    lane = lax.broadcasted_iota(I32, (1, LANES), 1)
    route = route_ref[...]
    ffn = None
    for kk in range(TOP_K):
        gate = jnp.sum(jnp.where(lane == kk, route, 0.0), axis=1, keepdims=True)
        term = gate * ybuf[slot, kk]
        ffn = term if ffn is None else ffn + term
    y = dn_alpha * x1_ref[...] + (1.0 + g2_ref[0]) * ffn
    o_ref[...] = _layer_norm(y, lng_ref[...], lnb_ref[...])


def _combine_call(pos3, y_rows, x1, route, g2, lng, lnb, *, seq, tm, dn_alpha):
    n, d = x1.shape
    nblk_s = seq // tm
    nblk = n // tm
    return pl.pallas_call(
        functools.partial(_combine_kernel, tm=tm, dn_alpha=dn_alpha),
        out_shape=jax.ShapeDtypeStruct((n, d), F32),
        grid=(nblk,),
        in_specs=[
            pl.BlockSpec((1, 1, TOP_K * tm), lambda i: (i, 0, 0), memory_space=pltpu.SMEM),
            pl.BlockSpec((1, 1, TOP_K * tm), lambda i: (jnp.minimum(i + 1, nblk - 1), 0, 0),
                         memory_space=pltpu.SMEM),
            pl.BlockSpec(memory_space=pl.ANY),
            pl.BlockSpec((tm, d), lambda i: (i, 0)),
            pl.BlockSpec((tm, LANES), lambda i: (i, 0)),
            pl.BlockSpec((1, 1, d), lambda i: (i // nblk_s, 0, 0)),
            pl.BlockSpec((1, d), lambda i: (0, 0)),
            pl.BlockSpec((1, d), lambda i: (0, 0)),
        ],
        out_specs=pl.BlockSpec((tm, d), lambda i: (i, 0)),
        scratch_shapes=[pltpu.VMEM((2, TOP_K, tm, d), F32), pltpu.SemaphoreType.DMA((2,))],
        compiler_params=_cparams(("arbitrary",)),
        name="moe_combine_ln",
    )(pos3, pos3, y_rows, x1, route, g2, lng, lnb)


def _routing(route, n):
    e_flat = route[:, TOP_K:2 * TOP_K].astype(I32).reshape(-1)
    na = n * TOP_K
    order = jnp.argsort(e_flat)
    inv = jnp.argsort(order)
    experts = jnp.arange(N_EXPERTS, dtype=I32)
    counts = jnp.sum((e_flat[:, None] == experts[None, :]).astype(I32), axis=0)
    starts = jnp.cumsum(counts) - counts
    padded = (counts + MOE_BLOCK - 1) // MOE_BLOCK * MOE_BLOCK
    padded_ends = jnp.cumsum(padded)
    padded_starts = padded_ends - padded
    n_rows = -(-na // MOE_BLOCK) * MOE_BLOCK + N_EXPERTS * MOE_BLOCK
    n_blocks = n_rows // MOE_BLOCK
    blk_start = jnp.arange(n_blocks, dtype=I32) * MOE_BLOCK
    block_expert = jnp.minimum(jnp.sum((padded_ends[None, :] <= blk_start[:, None]).astype(I32), axis=1),
                               N_EXPERTS - 1)
    nvalid = jnp.clip(padded_starts[block_expert] + counts[block_expert] - blk_start, 0, MOE_BLOCK)
    nvalid = jnp.where(blk_start < padded_ends[-1], nvalid, 0).astype(I32)
    row_e = jnp.repeat(block_expert, MOE_BLOCK)
    j = jnp.arange(n_rows, dtype=I32) - padded_starts[row_e]
    src = jnp.clip(starts[row_e] + j, 0, na - 1)
    row_tok = jnp.where(j < counts[row_e], order[src] // TOP_K, n).astype(I32)
    pos_flat = (padded_starts[e_flat] + inv - starts[e_flat]).astype(I32)
    return block_expert, nvalid, row_tok.reshape(n_blocks, 1, MOE_BLOCK), pos_flat.reshape(n, TOP_K)


def _rope_tables(seq):
    pos = jnp.arange(seq, dtype=F32)
    inv_freq = ROPE_THETA ** (-jnp.arange(0, ROT_DIM, 2, dtype=F32) / ROT_DIM)
    ang = pos[:, None] * inv_freq[None, :]
    cos, sin = jnp.cos(ang), jnp.sin(ang)
    half = ROT_DIM // 2
    ones = jnp.ones((seq, DA_QK - ROT_DIM), F32)
    zeros = jnp.zeros((seq, DA_QK - ROT_DIM), F32)
    zh = jnp.zeros((seq, half), F32)
    cos_g = jnp.concatenate([cos, cos, ones], axis=1)
    sinm_g = jnp.concatenate([-sin, zh, zeros], axis=1)
    sinp_g = jnp.concatenate([zh, sin, zeros], axis=1)
    rep = LANES // DA_QK
    return jnp.tile(cos_g, (1, rep)), jnp.tile(sinm_g, (1, rep)), jnp.tile(sinp_g, (1, rep))


def kernel(x, c, w_ada, b_ada, w_in, conv_w, gdn_a_log, gdn_dt_bias, gdn_norm_w, da_lambda, da_subln_w, w_out,
           ln1_g, ln1_b, w_router, b_router, w_gu, b_gu, w_down, b_down, ln2_g, ln2_b):
    batch, seq, d = x.shape
    depth = w_ada.shape[0]
    n = batch * seq
    dn_alpha = (2 * depth) ** 0.25
    tm = min(256, seq)
    tq = min(256, seq)
    tt = min(GDN_TT, seq)
    tc = min(128, seq)

    mod = _ada_call(c, w_ada, b_ada)
    wgu_all, wd_all = _wprep_call(w_gu, w_down)
    cos_t, sinm_t, sinp_t = _rope_tables(seq)
    x2 = x.reshape(n, d)
    for l in range(depth):
        lambda_init = 0.8 - 0.6 * math.exp(-0.3 * l)
        sh1, sc1, g1, sh2, sc2, g2 = [mod[l, :, j * d:(j + 1) * d].reshape(batch, 1, d) for j in range(6)]
        wl = w_in[l]
        p_sb = 3 * SB_W
        p_g = p_sb + 2 * GDN_KW + 2 * GDN_W
        p_ab = p_g + 2 * GDN_HEADS
        w_main = jnp.concatenate([wl[:, :p_g], wl[:, p_ab:]], axis=1).astype(BF16)
        w_ab = jnp.pad(wl[:, p_g:p_ab], ((0, 0), (0, LANES - 2 * GDN_HEADS))).astype(BF16)
        w_abt = wl[:, p_g:p_ab].T.astype(BF16)
        prow = jnp.zeros((8, LANES), F32).at[0, :GDN_HEADS].set(gdn_a_log[l]).at[1, :GDN_HEADS].set(gdn_dt_bias[l])
        pcol = jnp.zeros((8, LANES), F32).at[:GDN_HEADS, 0].set(gdn_a_log[l]).at[:GDN_HEADS, 1].set(gdn_dt_bias[l])
        nw2 = gdn_norm_w[l].reshape(1, GDN_DV)
        subw2 = jnp.tile(da_subln_w[l], LANES // DA_V).reshape(1, LANES)
        wr = jnp.pad(w_router[l], ((0, 0), (0, LANES - N_EXPERTS))).astype(BF16)
        br = jnp.pad(b_router[l], (0, LANES - N_EXPERTS)).reshape(1, LANES)
        bgu = jnp.concatenate([b_gu[l][..., 0::2], b_gu[l][..., 1::2]], axis=-1).reshape(N_EXPERTS, 1, 2 * D_FF)
        bd = b_down[l].reshape(N_EXPERTS, 1, d)

        sbq, sbk, sbv, g4, daq, dak, dav, gab, gabt = _inproj_call(
            x2, sc1, sh1, w_main, w_ab, w_abt, cos_t, sinm_t, sinp_t, seq=seq, tm=tm)
        o_sb = _sb_call(sbq, sbk, sbv, batch=batch, seq=seq, tq=tq)
        o_gdn = _gdn_call(g4, gab, gabt, conv_w[l], prow, pcol, nw2, batch=batch, seq=seq, tt=tt)
        o_da = _da_call(da_lambda[l], daq, dak, dav, subw2, batch=batch, seq=seq, tq=tq, lambda_init=lambda_init)
        x1, h2, route = _outproj_call(
            x2, o_sb, o_gdn, o_da, w_out[l].astype(BF16), g1, ln1_g[l].reshape(1, d), ln1_b[l].reshape(1, d),
            sc2, sh2, wr, br, seq=seq, tm=tm, dn_alpha=dn_alpha)
        block_expert, nvalid, row_tok3, pos = _routing(route, n)
        y_rows = _moe_call(block_expert, nvalid, row_tok3, h2, wgu_all, bgu, wd_all, bd, layer=l)
        pos3 = pos.reshape(n // tc, tc, TOP_K).transpose(0, 2, 1).reshape(n // tc, 1, TOP_K * tc)
        x2 = _combine_call(pos3, y_rows, x1, route, g2, ln2_g[l].reshape(1, d), ln2_b[l].reshape(1, d),
                           seq=seq, tm=tc, dn_alpha=dn_alpha)
    return x2.reshape(batch, seq, d)
```

```python
import functools
import math

import jax
import jax.numpy as jnp
from jax import lax
from jax.experimental import pallas as pl
from jax.experimental.pallas import tpu as pltpu

F32 = jnp.float32
BF16 = jnp.bfloat16
I32 = jnp.int32

LANES = 128
VMEM_LIMIT = 56 * 1024 * 1024

D_MODEL = 1024
SB_HEADS = 4
SB_DIM = 64
GDN_HEADS = 4
GDN_DK = 128
GDN_DV = 128
DA_HEADS = 4
DA_V = 64
DA_QK = 32
ROT_DIM = DA_QK // 4
ROPE_THETA = 500000.0
SB_W = SB_HEADS * SB_DIM
GDN_KW = GDN_HEADS * GDN_DK
GDN_W = GDN_HEADS * GDN_DV
DA_W = DA_HEADS * DA_V
CONV_K = 4
N_EXPERTS = 32
TOP_K = 4
D_FF = D_MODEL
SWIGLU_ALPHA = 1.702
SWIGLU_LIMIT = 7.0
MOE_BLOCK = 256
LN_EPS = 1e-5
RMS_EPS = 1e-6
NEG_BIG = -1e30
LOG2E = math.log2(math.e)
DMA_UNROLL = 8

C_SB = 0
C_G4 = 3 * SB_W
C_DQK = C_G4 + 4 * GDN_W
C_DV = C_DQK + 2 * DA_W
W_MAIN = C_DV + DA_W


def _dot(a, b):
    return jnp.dot(a, b, preferred_element_type=F32)


def _dot_nt(a, b):
    return lax.dot_general(a, b, (((1,), (1,)), ((), ())), preferred_element_type=F32)


def _dot_tn(a, b):
    return lax.dot_general(a, b, (((0,), (0,)), ((), ())), preferred_element_type=F32)


def _split3(x):
    h1 = x.astype(BF16)
    r1 = x - h1.astype(F32)
    h2 = r1.astype(BF16)
    h3 = (r1 - h2.astype(F32)).astype(BF16)
    return h1, h2, h3


def _mm(a, b, passes):
    ah = a.astype(BF16)
    bh = b.astype(BF16)
    if passes == 1:
        return _dot(ah, bh)
    al = (a - ah.astype(F32)).astype(BF16)
    bl = (b - bh.astype(F32)).astype(BF16)
    return _dot(ah, bh) + (_dot(al, bh) + _dot(ah, bl))


def _silu(x):
    return x * jax.nn.sigmoid(x)


def _softplus(x):
    return jnp.maximum(x, 0.0) + jnp.log(1.0 + jnp.exp(-jnp.abs(x)))


def _cparams(sem):
    return pltpu.CompilerParams(dimension_semantics=sem, vmem_limit_bytes=VMEM_LIMIT)


def _ada_kernel(c_ref, w_ref, b_ref, o_ref):
    cond = _silu(c_ref[...])
    w = w_ref[0]
    c1, c2, c3 = _split3(cond)
    w1, w2, w3 = _split3(w)
    acc = _dot(c1, w1) + (_dot(c1, w2) + _dot(c2, w1)) + (_dot(c2, w2) + _dot(c1, w3) + _dot(c3, w1))
    o_ref[0] = acc + b_ref[0]


def _ada_call(c, w_ada, b_ada):
    depth, d, d6 = w_ada.shape
    b = c.shape[0]
    nj = d6 // d
    return pl.pallas_call(
        _ada_kernel,
        out_shape=jax.ShapeDtypeStruct((depth, b, d6), F32),
        grid=(depth, nj),
        in_specs=[
            pl.BlockSpec((b, d), lambda l, j: (0, 0)),
            pl.BlockSpec((1, d, d), lambda l, j: (l, 0, j)),
            pl.BlockSpec((1, 1, d), lambda l, j: (l, 0, j)),
        ],
        out_specs=pl.BlockSpec((1, b, d), lambda l, j: (l, 0, j)),
        compiler_params=_cparams(("parallel", "parallel")),
        name="ada_mod",
    )(c, w_ada, b_ada.reshape(depth, 1, d6))


def _inproj_kernel(x_ref, sc_ref, sh_ref, w_ref, wab_ref, wabt_ref, cos_ref, sinm_ref, sinp_ref,
                   sbq_ref, sbk_ref, sbv_ref, g4_ref, daq_ref, dak_ref, dav_ref, gab_ref, gabt_ref):
    h = (x_ref[...] * (1.0 + sc_ref[0]) + sh_ref[0]).astype(BF16)

    def seg(a, n):
        return _dot(h, w_ref[:, a:a + n])

    sbq_ref[...] = (seg(C_SB, SB_W) * (SB_DIM ** -0.5 * LOG2E)).astype(BF16)
    sbk_ref[...] = seg(C_SB + SB_W, SB_W).astype(BF16)
    sbv_ref[...] = seg(C_SB + 2 * SB_W, SB_W).astype(BF16)
    for j in range(4):
        g4_ref[:, j * GDN_W:(j + 1) * GDN_W] = seg(C_G4 + j * GDN_W, GDN_W)
    cos = cos_ref[...]
    sinm = sinm_ref[...]
    sinp = sinp_ref[...]
    for ref, base, scale in ((daq_ref, C_DQK, DA_QK ** -0.5 * LOG2E), (dak_ref, C_DQK + DA_W, 1.0)):
        for j in range(DA_W // LANES):
            a = seg(base + j * LANES, LANES)
            r = a * cos + pltpu.roll(a, LANES - ROT_DIM // 2, 1) * sinm + pltpu.roll(a, ROT_DIM // 2, 1) * sinp
            ref[:, j * LANES:(j + 1) * LANES] = (r * scale).astype(BF16)
    dav_ref[...] = seg(C_DV, DA_W).astype(BF16)
    gab_ref[...] = _dot(h, wab_ref[...])
    gabt_ref[...] = _dot_nt(wabt_ref[...], h)


def _inproj_call(x2, sc, sh, w_main, w_ab, w_abt, cos_t, sinm_t, sinp_t, *, seq, tm):
    n, d = x2.shape
    nblk_s = seq // tm
    row = lambda i: (i, 0)
    bat = lambda i: (i // nblk_s, 0, 0)
    pos = lambda i: (i % nblk_s, 0)
    const = lambda i: (0, 0)
    outs = [
        (SB_W, BF16), (SB_W, BF16), (SB_W, BF16), (4 * GDN_W, F32),
        (DA_W, BF16), (DA_W, BF16), (DA_W, BF16), (LANES, F32),
    ]
    out_shape = [jax.ShapeDtypeStruct((n, w), dt) for w, dt in outs] + [jax.ShapeDtypeStruct((8, n), F32)]
    out_specs = [pl.BlockSpec((tm, w), row) for w, _ in outs] + [pl.BlockSpec((8, tm), lambda i: (0, i))]
    return pl.pallas_call(
        _inproj_kernel,
        out_shape=out_shape,
        grid=(n // tm,),
        in_specs=[
            pl.BlockSpec((tm, d), row),
            pl.BlockSpec((1, 1, d), bat),
            pl.BlockSpec((1, 1, d), bat),
            pl.BlockSpec(w_main.shape, const),
            pl.BlockSpec(w_ab.shape, const),
            pl.BlockSpec(w_abt.shape, const),
            pl.BlockSpec((tm, LANES), pos),
            pl.BlockSpec((tm, LANES), pos),
            pl.BlockSpec((tm, LANES), pos),
        ],
        out_specs=out_specs,
        compiler_params=_cparams(("parallel",)),
        name="inproj",
    )(x2, sc, sh, w_main, w_ab, w_abt, cos_t, sinm_t, sinp_t)


ATT_RC = 32


def _att_pipeline(i, stage_a, stage_b, stage_c):
    stage_a(0, i)
    stage_a(1, jnp.maximum(i - 1, 0))
    stage_b(0, 0, True)
    stage_c(0, i, True)
    stage_b(1, 1, False)
    stage_a(0, jnp.maximum(i - 2, 0))

    def step(parity, j):
        stage_c(1 - parity, j, False)
        stage_b(parity, parity, False)
        stage_a(1 - parity, jnp.maximum(j - 2, 0))

    def body(pp, _):
        j = i - 1 - 2 * pp
        step(0, j)
        step(1, j - 1)
        return 0

    lax.fori_loop(0, (i + 1) // 2, body, 0)


def _sb_kernel(q_ref, k_ref, v_ref, o_ref, z0, z1, t0, t1, b0, b1, tot0, tot1, hi0, hi1, lo0, lo1, w0, w1, carry_scr,
               acc_scr, *, tq):
    z_scr, t_scr, b_scr, tot_scr = (z0, z1), (t0, t1), (b0, b1), (tot0, tot1)
    hi_scr, lo_scr, w_scr = (hi0, hi1), (lo0, lo1), (w0, w1)
    i = pl.program_id(2)
    q = q_ref[...]
    lane = lax.broadcasted_iota(I32, (1, LANES), 1)
    rows = lax.broadcasted_iota(I32, (tq, tq), 0)
    cols = lax.broadcasted_iota(I32, (tq, tq), 1)
    u = jnp.where(cols < rows, 1.0, 0.0).astype(BF16)
    qhs = [jnp.where((lane >= SB_DIM * hh) & (lane < SB_DIM * (hh + 1)), q, jnp.zeros_like(q)) for hh in range(2)]
    crow = lax.broadcasted_iota(I32, (ATT_RC, tq), 0)
    ccol = lax.broadcasted_iota(I32, (ATT_RC, tq), 1)
    chunks = [slice(r * ATT_RC, (r + 1) * ATT_RC) for r in range(tq // ATT_RC)]
    carry_scr[...] = jnp.zeros_like(carry_scr)
    acc_scr[...] = jnp.zeros_like(acc_scr)

    def stage_a(zslot, j):
        kb = k_ref[pl.ds(pl.multiple_of(j * tq, tq), tq), :]
        for hh in range(2):
            z_scr[zslot][hh] = _dot_nt(qhs[hh], kb)

    def stage_b(zslot, tslot, masked):
        for hh in range(2):
            for r, rs in enumerate(chunks):
                z = z_scr[zslot][hh, rs, :]
                sp = jnp.maximum(z, 0.0) + jnp.log2(1.0 + jnp.exp2(-jnp.abs(z)))
                lk = -sp
                if masked:
                    lk = jnp.where(ccol < crow + r * ATT_RC, lk, 0.0)
                hi = lk.astype(BF16)
                hi_scr[tslot][hh, rs, :] = hi
                lo_scr[tslot][hh, rs, :] = (lk - hi.astype(F32)).astype(BF16)
                t_scr[tslot][hh, rs, :] = z - sp
                tot_scr[tslot][hh, rs, :] = jnp.broadcast_to(jnp.sum(lk, axis=1, keepdims=True), (ATT_RC, LANES))
        for hh in range(2):
            b_scr[tslot][hh] = _dot(hi_scr[tslot][hh], u) + _dot(lo_scr[tslot][hh], u)

    def stage_c(tslot, j, masked):
        vb = v_ref[pl.ds(pl.multiple_of(jnp.maximum(j, 0) * tq, tq), tq), :]
        vb = jnp.where(j >= 0, vb, jnp.zeros_like(vb))
        for hh in range(2):
            for r, rs in enumerate(chunks):
                carry = carry_scr[hh, rs, :]
                w = jnp.exp2(t_scr[tslot][hh, rs, :] + b_scr[tslot][hh, rs, :]
                             + jnp.concatenate([carry] * (tq // LANES), axis=1))
                if masked:
                    w = jnp.where(ccol < crow + r * ATT_RC, w, 0.0)
                w_scr[tslot][hh, rs, :] = w.astype(BF16)
                carry_scr[hh, rs, :] = carry + tot_scr[tslot][hh, rs, :]
        for hh in range(2):
            acc_scr[hh] += _dot(w_scr[tslot][hh], vb)

    _att_pipeline(i, stage_a, stage_b, stage_c)
    o_ref[...] = jnp.where(lane < SB_DIM, acc_scr[0], acc_scr[1]).astype(BF16)


def _sb_call(q, k, v, *, batch, seq, tq):
    n = q.shape[0]
    nq = seq // tq
    return pl.pallas_call(
        functools.partial(_sb_kernel, tq=tq),
        out_shape=jax.ShapeDtypeStruct((n, SB_W), BF16),
        grid=(batch, SB_W // LANES, nq),
        in_specs=[
            pl.BlockSpec((tq, LANES), lambda b, p, i: (b * nq + i, p)),
            pl.BlockSpec((seq, LANES), lambda b, p, i: (b, p)),
            pl.BlockSpec((seq, LANES), lambda b, p, i: (b, p)),
        ],
        out_specs=pl.BlockSpec((tq, LANES), lambda b, p, i: (b * nq + i, p)),
        scratch_shapes=[pltpu.VMEM((2, tq, tq), F32)] * 6 + [
            pltpu.VMEM((2, tq, LANES), F32),
            pltpu.VMEM((2, tq, LANES), F32),
        ] + [pltpu.VMEM((2, tq, tq), BF16)] * 6 + [
            pltpu.VMEM((2, tq, LANES), F32),
            pltpu.VMEM((2, tq, LANES), F32),
        ],
        compiler_params=_cparams(("parallel", "parallel", "arbitrary")),
        name="sb_attn",
    )(q, k, v)


def _da_kernel(lam_ref, q_ref, k_ref, v_ref, subw_ref, o_ref, s0, s1, p0, p1, al0, al1, m_scr, acc_scr,
               *, tq, lambda_init):
    s_scr, p_scr, al_scr = (s0, s1), (p0, p1), (al0, al1)
    i = pl.program_id(2)
    lp = lam_ref[...]
    lam = (jnp.exp(jnp.sum(lp[0:1] * lp[1:2], axis=1, keepdims=True))
           - jnp.exp(jnp.sum(lp[2:3] * lp[3:4], axis=1, keepdims=True)) + lambda_init)
    q = q_ref[...]
    lane = lax.broadcasted_iota(I32, (1, LANES), 1)
    first = lane < DA_V
    n_maps = LANES // DA_QK
    qms = [jnp.where((lane >= DA_QK * g) & (lane < DA_QK * (g + 1)), q, jnp.zeros_like(q)) for g in range(n_maps)]
    crow = lax.broadcasted_iota(I32, (ATT_RC, tq), 0)
    ccol = lax.broadcasted_iota(I32, (ATT_RC, tq), 1)
    chunks = [slice(r * ATT_RC, (r + 1) * ATT_RC) for r in range(tq // ATT_RC)]
    m_scr[...] = jnp.full(m_scr.shape, NEG_BIG, F32)
    acc_scr[...] = jnp.zeros_like(acc_scr)

    def stage_a(sslot, j):
        kb = k_ref[pl.ds(pl.multiple_of(j * tq, tq), tq), :]
        for g in range(n_maps):
            s_scr[sslot][g] = _dot_nt(qms[g], kb)

    def stage_b(sslot, pslot, masked):
        for g in range(n_maps):
            for r, rs in enumerate(chunks):
                s = s_scr[sslot][g, rs, :]
                if masked:
                    s = jnp.where(ccol <= crow + r * ATT_RC, s, NEG_BIG)
                m_old = m_scr[g, rs, :]
                m_new = jnp.maximum(m_old, jnp.broadcast_to(jnp.max(s, axis=1, keepdims=True), (ATT_RC, LANES)))
                al_scr[pslot][g, rs, :] = jnp.exp2(m_old - m_new)
                p_scr[pslot][g, rs, :] = jnp.exp2(s - jnp.concatenate([m_new] * (tq // LANES), axis=1)).astype(BF16)
                m_scr[g, rs, :] = m_new

    def stage_c(pslot, j, masked):
        del masked
        vb = v_ref[pl.ds(pl.multiple_of(jnp.maximum(j, 0) * tq, tq), tq), :]
        one = jnp.where(j >= 0, jnp.ones_like(vb), jnp.zeros_like(vb))
        vb = jnp.where(j >= 0, vb, jnp.zeros_like(vb))
        vh = (jnp.where(first, vb, one), jnp.where(first, one, vb))
        for g in range(n_maps):
            acc_scr[g] = al_scr[pslot][g] * acc_scr[g] + _dot(p_scr[pslot][g], vh[g // 2])

    _att_pipeline(i, stage_a, stage_b, stage_c)
    maps = [acc_scr[g] / pltpu.roll(acc_scr[g], DA_V, 1) for g in range(n_maps)]
    o = jnp.where(first, maps[0] - lam * maps[1], maps[2] - lam * maps[3])
    sq = o * o
    ms0 = jnp.sum(jnp.where(first, sq, 0.0), axis=1, keepdims=True) * (1.0 / DA_V)
    ms1 = jnp.sum(jnp.where(first, 0.0, sq), axis=1, keepdims=True) * (1.0 / DA_V)
    ms = jnp.where(first, ms0, ms1)
    o_ref[...] = ((o * lax.rsqrt(ms + RMS_EPS) * subw_ref[...]) * (1.0 - lambda_init)).astype(BF16)


def _da_call(lam_p, q, k, v, subw2, *, batch, seq, tq, lambda_init):
    n = q.shape[0]
    nq = seq // tq
    n_maps = LANES // DA_QK
    return pl.pallas_call(
        functools.partial(_da_kernel, tq=tq, lambda_init=lambda_init),
        out_shape=jax.ShapeDtypeStruct((n, DA_W), BF16),
        grid=(batch, DA_W // LANES, nq),
        in_specs=[
            pl.BlockSpec(lam_p.shape, lambda b, p, i: (0, 0)),
            pl.BlockSpec((tq, LANES), lambda b, p, i: (b * nq + i, p)),
            pl.BlockSpec((seq, LANES), lambda b, p, i: (b, p)),
            pl.BlockSpec((seq, LANES), lambda b, p, i: (b, p)),
            pl.BlockSpec((1, LANES), lambda b, p, i: (0, 0)),
        ],
        out_specs=pl.BlockSpec((tq, LANES), lambda b, p, i: (b * nq + i, p)),
        scratch_shapes=[
            pltpu.VMEM((n_maps, tq, tq), F32),
            pltpu.VMEM((n_maps, tq, tq), F32),
            pltpu.VMEM((n_maps, tq, tq), BF16),
            pltpu.VMEM((n_maps, tq, tq), BF16),
            pltpu.VMEM((n_maps, tq, LANES), F32),
            pltpu.VMEM((n_maps, tq, LANES), F32),
            pltpu.VMEM((n_maps, tq, LANES), F32),
            pltpu.VMEM((n_maps, tq, LANES), F32),
        ],
        compiler_params=_cparams(("parallel", "parallel", "arbitrary")),
        name="da_attn",
    )(lam_p, q, k, v, subw2)


GDN_CHUNK = 128
CONV_PAD = 8
GDN_TT = 512


def _gdn_kernel(q_ref, k_ref, v_ref, z_ref, ab_ref, abt_ref, cwq_ref, cwk_ref, cwv_ref, prow_ref, pcol_ref,
                nw_ref, o_ref, state_ref, cq_ref, ck_ref, cv_ref, *, tt):
    hd = pl.program_id(1)
    i = pl.program_id(2)
    c = GDN_CHUNK
    lane = lax.broadcasted_iota(I32, (1, LANES), 1)
    sub8 = lax.broadcasted_iota(I32, (8, 1), 0)
    rows = lax.broadcasted_iota(I32, (c, c), 0)
    cols = lax.broadcasted_iota(I32, (c, c), 1)
    lower = cols <= rows
    strict = cols < rows
    eye = jnp.where(cols == rows, 1.0, 0.0)
    ltri = jnp.where(lower, 1.0, 0.0).astype(BF16)
    utri = jnp.where(rows <= cols, 1.0, 0.0).astype(BF16)

    @pl.when(i == 0)
    def _():
        state_ref[...] = jnp.zeros_like(state_ref)
        for r in (cq_ref, ck_ref, cv_ref):
            r[0:CONV_PAD, :] = jnp.zeros((CONV_PAD, LANES), F32)

    @pl.when(i > 0)
    def _():
        for r in (cq_ref, ck_ref, cv_ref):
            r[0:CONV_PAD, :] = r[tt:tt + CONV_PAD, :]

    def conv_silu(x_ref, buf_ref, w_ref):
        buf_ref[CONV_PAD:CONV_PAD + tt, :] = x_ref[...]
        acc = None
        for j in range(CONV_K):
            off = CONV_PAD - (CONV_K - 1) + j
            term = buf_ref[off:off + tt, :] * w_ref[j:j + 1, :]
            acc = term if acc is None else acc + term
        return _silu(acc)

    qc = conv_silu(q_ref, cq_ref, cwq_ref)
    kc = conv_silu(k_ref, ck_ref, cwk_ref)
    vc = conv_silu(v_ref, cv_ref, cwv_ref)
    qn = qc * lax.rsqrt(jnp.sum(qc * qc, axis=1, keepdims=True) + 1e-6) * (GDN_DK ** -0.5)
    kn = kc * lax.rsqrt(jnp.sum(kc * kc, axis=1, keepdims=True) + 1e-6)

    ab = ab_ref[...]
    g_all = -jnp.exp(prow_ref[0:1, :]) * _softplus(ab + prow_ref[1:2, :])
    g_col = jnp.sum(jnp.where(lane == hd, g_all, 0.0), axis=1, keepdims=True)
    beta_col = jnp.sum(jnp.where(lane == hd + GDN_HEADS, jax.nn.sigmoid(ab), 0.0), axis=1, keepdims=True)
    abt = abt_ref[...]
    g_allt = -jnp.exp(pcol_ref[:, 0:1]) * _softplus(abt + pcol_ref[:, 1:2])
    g_row = jnp.sum(jnp.where(sub8 == hd, g_allt, 0.0), axis=0, keepdims=True)

    nc = tt // c
    sls = [slice(ci * c, (ci + 1) * c) for ci in range(nc)]
    gcs, decays, kbs, ms, attns = [], [], [], [], []
    for sl in sls:
        g1, g2, g3 = _split3(jnp.broadcast_to(g_col[sl], (c, LANES)))
        gc = _dot(ltri, g1) + _dot(ltri, g2) + _dot(ltri, g3)
        r1, r2, r3 = _split3(jnp.broadcast_to(g_row[:, sl], (8, c)))
        gcr = (_dot(r1, utri) + _dot(r2, utri) + _dot(r3, utri))[0:1, :]
        gcs.append(gc)
        decays.append(jnp.where(lower, jnp.exp(jnp.where(lower, gc - gcr, 0.0)), 0.0))
    for sl, decay in zip(sls, decays):
        kb = kn[sl] * beta_col[sl]
        kt = kn[sl].T
        kbs.append(kb)
        ms.append(jnp.where(strict, _mm(kb, kt, 1) * decay, 0.0))
        attns.append(jnp.where(lower, _mm(qn[sl], kt, 1) * decay, 0.0))
    ps = [-m for m in ms]
    tinvs = [eye + p for p in ps]
    for _ in range(int(math.log2(c)) - 2):
        ps = [_mm(p, p, 1) for p in ps]
        tinvs = [t + _mm(t, p, 1) for t, p in zip(tinvs, ps)]
    resids = [eye - _mm(eye + m, t, 3) for m, t in zip(ms, tinvs)]
    tinvs = [t + _mm(t, r, 1) for t, r in zip(tinvs, resids)]
    egs = [jnp.exp(gc) for gc in gcs]
    sols = [_mm(t, jnp.concatenate([vc[sl] * beta_col[sl], kb * eg], axis=1), 1)
            for t, sl, kb, eg in zip(tinvs, sls, kbs, egs)]
    for ci, sl in enumerate(sls):
        gc, eg = gcs[ci], egs[ci]
        u_c = sols[ci][:, :GDN_DV]
        w_c = sols[ci][:, GDN_DV:]
        st = state_ref[...]
        v_new = u_c - _mm(w_c, st, 1)
        o = _mm(qn[sl] * eg, st, 1) + _mm(attns[ci], v_new, 1)
        g_last = gc[c - 1:c, :]
        kd = kn[sl] * jnp.exp(g_last - gc)
        state_ref[...] = st * jnp.exp(g_last) + _mm(kd.T, v_new, 1)
        on = o * lax.rsqrt(jnp.mean(o * o, axis=1, keepdims=True) + RMS_EPS) * nw_ref[...]
        o_ref[sl, :] = (on * _silu(z_ref[sl, :])).astype(BF16)


def _gdn_call(g4, gab, gabt, conv_w, prow, pcol, nw2, *, batch, seq, tt):
    n = g4.shape[0]
    ns = seq // tt
    tok = lambda off: (lambda b, h, i: (b * ns + i, off + h))
    cw = lambda off: (lambda b, h, i: (0, off + h))
    const = lambda b, h, i: (0, 0)
    return pl.pallas_call(
        functools.partial(_gdn_kernel, tt=tt),
        out_shape=jax.ShapeDtypeStruct((n, GDN_W), BF16),
        grid=(batch, GDN_HEADS, ns),
        in_specs=[
            pl.BlockSpec((tt, LANES), tok(0)),
            pl.BlockSpec((tt, LANES), tok(GDN_HEADS)),
            pl.BlockSpec((tt, LANES), tok(2 * GDN_HEADS)),
            pl.BlockSpec((tt, LANES), tok(3 * GDN_HEADS)),
            pl.BlockSpec((tt, LANES), lambda b, h, i: (b * ns + i, 0)),
            pl.BlockSpec((8, tt), lambda b, h, i: (0, b * ns + i)),
            pl.BlockSpec((CONV_K, LANES), cw(0)),
            pl.BlockSpec((CONV_K, LANES), cw(GDN_HEADS)),
            pl.BlockSpec((CONV_K, LANES), cw(2 * GDN_HEADS)),
            pl.BlockSpec(prow.shape, const),
            pl.BlockSpec(pcol.shape, const),
            pl.BlockSpec((1, LANES), const),
        ],
        out_specs=pl.BlockSpec((tt, LANES), tok(0)),
        scratch_shapes=[
            pltpu.VMEM((GDN_DK, GDN_DV), F32),
            pltpu.VMEM((tt + CONV_PAD, LANES), F32),
            pltpu.VMEM((tt + CONV_PAD, LANES), F32),
            pltpu.VMEM((tt + CONV_PAD, LANES), F32),
        ],
        compiler_params=_cparams(("parallel", "parallel", "arbitrary")),
        name="gdn",
    )(g4, g4, g4, g4, gab, gabt, conv_w, conv_w, conv_w, prow, pcol, nw2)


def _layer_norm(y, g, b):
    mu = jnp.mean(y, axis=1, keepdims=True)
    yc = y - mu
    var = jnp.mean(yc * yc, axis=1, keepdims=True)
    return yc * lax.rsqrt(var + LN_EPS) * g + b


def _outproj_kernel(x_ref, osb_ref, ogdn_ref, oda_ref, w_ref, g1_ref, lng_ref, lnb_ref, sc2_ref, sh2_ref,
                    wr_ref, br_ref, x1_ref, h2_ref, route_ref, *, dn_alpha):
    mix = (_dot(osb_ref[...], w_ref[0:SB_W, :]) + _dot(ogdn_ref[...], w_ref[SB_W:SB_W + GDN_W, :])
           + _dot(oda_ref[...], w_ref[SB_W + GDN_W:, :]))
    y = dn_alpha * x_ref[...] + (1.0 + g1_ref[0]) * mix
    x1 = _layer_norm(y, lng_ref[...], lnb_ref[...])
    x1_ref[...] = x1
    h2 = x1 * (1.0 + sc2_ref[0]) + sh2_ref[0]
    h2_ref[...] = h2
    lane = lax.broadcasted_iota(I32, (1, LANES), 1)
    logits = _dot(h2.astype(BF16), wr_ref[...]) + br_ref[...]
    logits = jnp.where(lane < N_EXPERTS, logits, NEG_BIG)
    vals, idxs = [], []
    for _ in range(TOP_K):
        mx = jnp.max(logits, axis=1, keepdims=True)
        ix = jnp.min(jnp.where(logits == mx, lane, LANES), axis=1, keepdims=True)
        vals.append(mx)
        idxs.append(ix)
        logits = jnp.where(lane == ix, NEG_BIG, logits)
    es = [jnp.exp(v - vals[0]) for v in vals]
    den = es[0] + es[1] + es[2] + es[3]
    route = jnp.zeros(route_ref.shape, F32)
    for kk in range(TOP_K):
        route = jnp.where(lane == kk, es[kk] / den, route)
        route = jnp.where(lane == TOP_K + kk, idxs[kk].astype(F32), route)
    route_ref[...] = route


def _outproj_call(x2, osb, ogdn, oda, w_out, g1, lng, lnb, sc2, sh2, wr, br, *, seq, tm, dn_alpha):
    n, d = x2.shape
    nblk_s = seq // tm
    row = lambda i: (i, 0)
    bat = lambda i: (i // nblk_s, 0, 0)
    const = lambda i: (0, 0)
    return pl.pallas_call(
        functools.partial(_outproj_kernel, dn_alpha=dn_alpha),
        out_shape=[jax.ShapeDtypeStruct((n, d), F32), jax.ShapeDtypeStruct((n, d), F32),
                   jax.ShapeDtypeStruct((n, LANES), F32)],
        grid=(n // tm,),
        in_specs=[
            pl.BlockSpec((tm, d), row),
            pl.BlockSpec((tm, SB_W), row),
            pl.BlockSpec((tm, GDN_W), row),
            pl.BlockSpec((tm, DA_W), row),
            pl.BlockSpec(w_out.shape, const),
            pl.BlockSpec((1, 1, d), bat),
            pl.BlockSpec((1, d), const),
            pl.BlockSpec((1, d), const),
            pl.BlockSpec((1, 1, d), bat),
            pl.BlockSpec((1, 1, d), bat),
            pl.BlockSpec(wr.shape, const),
            pl.BlockSpec((1, LANES), const),
        ],
        out_specs=[pl.BlockSpec((tm, d), row), pl.BlockSpec((tm, d), row), pl.BlockSpec((tm, LANES), row)],
        compiler_params=_cparams(("parallel",)),
        name="outproj_ln_router",
    )(x2, osb, ogdn, oda, w_out, g1, lng, lnb, sc2, sh2, wr, br)


PREP_ROWS = 256
PREP_COLS = 512


def _wprep_kernel(wgu_ref, wd_ref, ogu_ref, od_ref):
    rows = lax.broadcasted_iota(I32, (PREP_COLS, PREP_COLS // 2), 0)
    cols = lax.broadcasted_iota(I32, (PREP_COLS, PREP_COLS // 2), 1)
    sel_even = jnp.where(rows == 2 * cols, 1.0, 0.0).astype(BF16)
    sel_odd = jnp.where(rows == 2 * cols + 1, 1.0, 0.0).astype(BF16)
    half = PREP_COLS // 2
    for j in range(2 * D_FF // PREP_COLS):
        w = wgu_ref[0, 0, :, j * PREP_COLS:(j + 1) * PREP_COLS].astype(BF16)
        ogu_ref[0, 0, :, j * half:(j + 1) * half] = _dot(w, sel_even).astype(BF16)
        ogu_ref[0, 0, :, D_FF + j * half:D_FF + (j + 1) * half] = _dot(w, sel_odd).astype(BF16)
    od_ref[...] = wd_ref[...].astype(BF16)


def _wprep_call(w_gu, w_down):
    depth, ne, d, f2 = w_gu.shape
    nr = d // PREP_ROWS
    return pl.pallas_call(
        _wprep_kernel,
        out_shape=[jax.ShapeDtypeStruct(w_gu.shape, BF16), jax.ShapeDtypeStruct(w_down.shape, BF16)],
        grid=(depth, ne, nr),
        in_specs=[
            pl.BlockSpec((1, 1, PREP_ROWS, f2), lambda l, e, r: (l, e, r, 0)),
            pl.BlockSpec((1, 1, PREP_ROWS, d), lambda l, e, r: (l, e, r, 0)),
        ],
        out_specs=[
            pl.BlockSpec((1, 1, PREP_ROWS, f2), lambda l, e, r: (l, e, r, 0)),
            pl.BlockSpec((1, 1, PREP_ROWS, d), lambda l, e, r: (l, e, r, 0)),
        ],
        compiler_params=_cparams(("parallel", "parallel", "parallel")),
        name="expert_weight_prep",
    )(w_gu, w_down)


def _moe_kernel(bexp_ref, nvalid_ref, tokc_ref, tokn_ref, h_hbm, wgu_ref, bgu_ref, wd_ref, bd_ref, y_ref, xbuf0, xbuf1,
                sem, *, n_tok):
    i = pl.program_id(0)
    xbufs = (xbuf0, xbuf1)
    valid = nvalid_ref[i] > 0

    def start_row(tok_ref, r, slot):
        t = jnp.minimum(tok_ref[0, 0, r], n_tok - 1)
        pltpu.make_async_copy(h_hbm.at[pl.ds(t, 1)], xbufs[slot].at[pl.ds(r, 1)], sem.at[slot]).start()

    def wait_block(slot):
        pltpu.make_async_copy(h_hbm.at[pl.ds(0, MOE_BLOCK)], xbufs[slot], sem.at[slot]).wait()

    @pl.when((i == 0) & valid)
    def _():
        lax.fori_loop(0, MOE_BLOCK, lambda r, c: (start_row(tokc_ref, r, 0), c)[1], 0, unroll=DMA_UNROLL)

    for slot in range(2):
        @pl.when(valid & (i % 2 == slot))
        def _(slot=slot):
            wait_block(slot)
            xb = xbufs[slot][...].astype(BF16)
            hgu = _dot(xb, wgu_ref[0, 0]) + bgu_ref[0]
            for r in range(MOE_BLOCK):
                start_row(tokn_ref, r, 1 - slot)
            glu = jnp.minimum(hgu[:, :D_FF], SWIGLU_LIMIT)
            lin = jnp.clip(hgu[:, D_FF:], -SWIGLU_LIMIT, SWIGLU_LIMIT)
            act = glu * jax.nn.sigmoid(SWIGLU_ALPHA * glu) * (lin + 1.0)
            y_ref[...] = _dot(act.astype(BF16), wd_ref[0, 0]) + bd_ref[0]

    @pl.when(jnp.logical_not(valid))
    def _():
        y_ref[...] = jnp.zeros_like(y_ref)

    for slot in range(2):
        @pl.when(jnp.logical_not(valid) & (i > 0) & (nvalid_ref[jnp.maximum(i - 1, 0)] > 0) & (i % 2 == slot))
        def _(slot=slot):
            wait_block(slot)


def _moe_call(block_expert, nvalid, row_tok3, h2, wgu_all, bgu, wd_all, bd, *, layer):
    n_blocks = block_expert.shape[0]
    n, d = h2.shape
    wsel = lambda i, be, nv: (layer, be[i], 0, 0)
    bsel = lambda i, be, nv: (be[i], 0, 0)
    return pl.pallas_call(
        functools.partial(_moe_kernel, n_tok=n),
        out_shape=jax.ShapeDtypeStruct((n_blocks * MOE_BLOCK, d), F32),
        grid_spec=pltpu.PrefetchScalarGridSpec(
            num_scalar_prefetch=2,
            grid=(n_blocks,),
            in_specs=[
                pl.BlockSpec((1, 1, MOE_BLOCK), lambda i, be, nv: (i, 0, 0), memory_space=pltpu.SMEM),
                pl.BlockSpec((1, 1, MOE_BLOCK), lambda i, be, nv: (jnp.minimum(i + 1, n_blocks - 1), 0, 0),
                             memory_space=pltpu.SMEM),
                pl.BlockSpec(memory_space=pl.ANY),
                pl.BlockSpec((1, 1, d, 2 * D_FF), wsel),
                pl.BlockSpec((1, 1, 2 * D_FF), bsel),
                pl.BlockSpec((1, 1, D_FF, d), wsel),
                pl.BlockSpec((1, 1, d), bsel),
            ],
            out_specs=pl.BlockSpec((MOE_BLOCK, d), lambda i, be, nv: (i, 0)),
            scratch_shapes=[pltpu.VMEM((MOE_BLOCK, d), F32), pltpu.VMEM((MOE_BLOCK, d), F32),
                            pltpu.SemaphoreType.DMA((2,))],
        ),
        compiler_params=_cparams(("arbitrary",)),
        name="moe_experts",
    )(block_expert, nvalid, row_tok3, row_tok3, h2, wgu_all, bgu, wd_all, bd)


def _combine_kernel(posc_ref, posn_ref, y_hbm, x1_ref, route_ref, g2_ref, lng_ref, lnb_ref, o_ref, ybuf, sem,
                    *, tm, dn_alpha):
    i = pl.program_id(0)
    nb = pl.num_programs(0)
    slot = i % 2

    def issue(pos_ref, slot_):
        for kk in range(TOP_K):
            def body(r, carry, kk=kk):
                p = pos_ref[0, 0, kk * tm + r]
                pltpu.make_async_copy(y_hbm.at[pl.ds(p, 1)], ybuf.at[slot_, kk, pl.ds(r, 1)], sem.at[slot_]).start()
                return carry
            lax.fori_loop(0, tm, body, 0, unroll=DMA_UNROLL)

    @pl.when(i == 0)
    def _():
        issue(posc_ref, 0)

    @pl.when(i + 1 < nb)
    def _():
        issue(posn_ref, 1 - slot)

    for kk in range(TOP_K):
        pltpu.make_async_copy(y_hbm.at[pl.ds(0, tm)], ybuf.at[slot, kk], sem.at[slot]).wait()
    lane = lax.broadcasted_iota(I32, (1, LANES), 1)
    route = route_ref[...]
    ffn = None
    for kk in range(TOP_K):
        gate = jnp.sum(jnp.where(lane == kk, route, 0.0), axis=1, keepdims=True)
        term = gate * ybuf[slot, kk]
        ffn = term if ffn is None else ffn + term
    y = dn_alpha * x1_ref[...] + (1.0 + g2_ref[0]) * ffn
    o_ref[...] = _layer_norm(y, lng_ref[...], lnb_ref[...])


def _combine_call(pos3, y_rows, x1, route, g2, lng, lnb, *, seq, tm, dn_alpha):
    n, d = x1.shape
    nblk_s = seq // tm
    nblk = n // tm
    return pl.pallas_call(
        functools.partial(_combine_kernel, tm=tm, dn_alpha=dn_alpha),
        out_shape=jax.ShapeDtypeStruct((n, d), F32),
        grid=(nblk,),
        in_specs=[
            pl.BlockSpec((1, 1, TOP_K * tm), lambda i: (i, 0, 0), memory_space=pltpu.SMEM),
            pl.BlockSpec((1, 1, TOP_K * tm), lambda i: (jnp.minimum(i + 1, nblk - 1), 0, 0),
                         memory_space=pltpu.SMEM),
            pl.BlockSpec(memory_space=pl.ANY),
            pl.BlockSpec((tm, d), lambda i: (i, 0)),
            pl.BlockSpec((tm, LANES), lambda i: (i, 0)),
            pl.BlockSpec((1, 1, d), lambda i: (i // nblk_s, 0, 0)),
            pl.BlockSpec((1, d), lambda i: (0, 0)),
            pl.BlockSpec((1, d), lambda i: (0, 0)),
        ],
        out_specs=pl.BlockSpec((tm, d), lambda i: (i, 0)),
        scratch_shapes=[pltpu.VMEM((2, TOP_K, tm, d), F32), pltpu.SemaphoreType.DMA((2,))],
        compiler_params=_cparams(("arbitrary",)),
        name="moe_combine_ln",
    )(pos3, pos3, y_rows, x1, route, g2, lng, lnb)


def _routing(route, n):
    e_flat = route[:, TOP_K:2 * TOP_K].astype(I32).reshape(-1)
    na = n * TOP_K
    order = jnp.argsort(e_flat)
    inv = jnp.argsort(order)
    experts = jnp.arange(N_EXPERTS, dtype=I32)
    counts = jnp.sum((e_flat[:, None] == experts[None, :]).astype(I32), axis=0)
    starts = jnp.cumsum(counts) - counts
    padded = (counts + MOE_BLOCK - 1) // MOE_BLOCK * MOE_BLOCK
    padded_ends = jnp.cumsum(padded)
    padded_starts = padded_ends - padded
    n_rows = -(-na // MOE_BLOCK) * MOE_BLOCK + N_EXPERTS * MOE_BLOCK
    n_blocks = n_rows // MOE_BLOCK
    blk_start = jnp.arange(n_blocks, dtype=I32) * MOE_BLOCK
    block_expert = jnp.minimum(jnp.sum((padded_ends[None, :] <= blk_start[:, None]).astype(I32), axis=1),
                               N_EXPERTS - 1)
    nvalid = jnp.clip(padded_starts[block_expert] + counts[block_expert] - blk_start, 0, MOE_BLOCK)
    nvalid = jnp.where(blk_start < padded_ends[-1], nvalid, 0).astype(I32)
    row_e = jnp.repeat(block_expert, MOE_BLOCK)
    j = jnp.arange(n_rows, dtype=I32) - padded_starts[row_e]
    src = jnp.clip(starts[row_e] + j, 0, na - 1)
    row_tok = jnp.where(j < counts[row_e], order[src] // TOP_K, n).astype(I32)
    pos_flat = (padded_starts[e_flat] + inv - starts[e_flat]).astype(I32)
    return block_expert, nvalid, row_tok.reshape(n_blocks, 1, MOE_BLOCK), pos_flat.reshape(n, TOP_K)


def _rope_tables(seq):
    pos = jnp.arange(seq, dtype=F32)
    inv_freq = ROPE_THETA ** (-jnp.arange(0, ROT_DIM, 2, dtype=F32) / ROT_DIM)
    ang = pos[:, None] * inv_freq[None, :]
    cos, sin = jnp.cos(ang), jnp.sin(ang)
    half = ROT_DIM // 2
    ones = jnp.ones((seq, DA_QK - ROT_DIM), F32)
    zeros = jnp.zeros((seq, DA_QK - ROT_DIM), F32)
    zh = jnp.zeros((seq, half), F32)
    cos_g = jnp.concatenate([cos, cos, ones], axis=1)
    sinm_g = jnp.concatenate([-sin, zh, zeros], axis=1)
    sinp_g = jnp.concatenate([zh, sin, zeros], axis=1)
    rep = LANES // DA_QK
    return jnp.tile(cos_g, (1, rep)), jnp.tile(sinm_g, (1, rep)), jnp.tile(sinp_g, (1, rep))


def kernel(x, c, w_ada, b_ada, w_in, conv_w, gdn_a_log, gdn_dt_bias, gdn_norm_w, da_lambda, da_subln_w, w_out,
           ln1_g, ln1_b, w_router, b_router, w_gu, b_gu, w_down, b_down, ln2_g, ln2_b):
    batch, seq, d = x.shape
    depth = w_ada.shape[0]
    n = batch * seq
    dn_alpha = (2 * depth) ** 0.25
    tm = min(256, seq)
    tq = min(256, seq)
    tt = min(GDN_TT, seq)
    tc = min(128, seq)

    mod = _ada_call(c, w_ada, b_ada)
    wgu_all, wd_all = _wprep_call(w_gu, w_down)
    cos_t, sinm_t, sinp_t = _rope_tables(seq)
    x2 = x.reshape(n, d)
    for l in range(depth):
        lambda_init = 0.8 - 0.6 * math.exp(-0.3 * l)
        sh1, sc1, g1, sh2, sc2, g2 = [mod[l, :, j * d:(j + 1) * d].reshape(batch, 1, d) for j in range(6)]
        wl = w_in[l]
        p_sb = 3 * SB_W
        p_g = p_sb + 2 * GDN_KW + 2 * GDN_W
        p_ab = p_g + 2 * GDN_HEADS
        w_main = jnp.concatenate([wl[:, :p_g], wl[:, p_ab:]], axis=1).astype(BF16)
        w_ab = jnp.pad(wl[:, p_g:p_ab], ((0, 0), (0, LANES - 2 * GDN_HEADS))).astype(BF16)
        w_abt = wl[:, p_g:p_ab].T.astype(BF16)
        prow = jnp.zeros((8, LANES), F32).at[0, :GDN_HEADS].set(gdn_a_log[l]).at[1, :GDN_HEADS].set(gdn_dt_bias[l])
        pcol = jnp.zeros((8, LANES), F32).at[:GDN_HEADS, 0].set(gdn_a_log[l]).at[:GDN_HEADS, 1].set(gdn_dt_bias[l])
        nw2 = gdn_norm_w[l].reshape(1, GDN_DV)
        subw2 = jnp.tile(da_subln_w[l], LANES // DA_V).reshape(1, LANES)
        wr = jnp.pad(w_router[l], ((0, 0), (0, LANES - N_EXPERTS))).astype(BF16)
        br = jnp.pad(b_router[l], (0, LANES - N_EXPERTS)).reshape(1, LANES)
        bgu = jnp.concatenate([b_gu[l][..., 0::2], b_gu[l][..., 1::2]], axis=-1).reshape(N_EXPERTS, 1, 2 * D_FF)
        bd = b_down[l].reshape(N_EXPERTS, 1, d)

        sbq, sbk, sbv, g4, daq, dak, dav, gab, gabt = _inproj_call(
            x2, sc1, sh1, w_main, w_ab, w_abt, cos_t, sinm_t, sinp_t, seq=seq, tm=tm)
        o_sb = _sb_call(sbq, sbk, sbv, batch=batch, seq=seq, tq=tq)
        o_gdn = _gdn_call(g4, gab, gabt, conv_w[l], prow, pcol, nw2, batch=batch, seq=seq, tt=tt)
        o_da = _da_call(da_lambda[l], daq, dak, dav, subw2, batch=batch, seq=seq, tq=tq, lambda_init=lambda_init)
        x1, h2, route = _outproj_call(
            x2, o_sb, o_gdn, o_da, w_out[l].astype(BF16), g1, ln1_g[l].reshape(1, d), ln1_b[l].reshape(1, d),
            sc2, sh2, wr, br, seq=seq, tm=tm, dn_alpha=dn_alpha)
        block_expert, nvalid, row_tok3, pos = _routing(route, n)
        y_rows = _moe_call(block_expert, nvalid, row_tok3, h2, wgu_all, bgu, wd_all, bd, layer=l)
        pos3 = pos.reshape(n // tc, tc, TOP_K).transpose(0, 2, 1).reshape(n // tc, 1, TOP_K * tc)
        x2 = _combine_call(pos3, y_rows, x1, route, g2, ln2_g[l].reshape(1, d), ln2_b[l].reshape(1, d),
                           seq=seq, tm=tc, dn_alpha=dn_alpha)
    return x2.reshape(batch, seq, d)
```

```python
import functools
import math

import jax
import jax.numpy as jnp
from jax import lax
from jax.experimental import pallas as pl
from jax.experimental.pallas import tpu as pltpu

F32 = jnp.float32
BF16 = jnp.bfloat16
I32 = jnp.int32

LANES = 128
VMEM_LIMIT = 56 * 1024 * 1024

D_MODEL = 1024
SB_HEADS = 4
SB_DIM = 64
GDN_HEADS = 4
GDN_DK = 128
GDN_DV = 128
DA_HEADS = 4
DA_V = 64
DA_QK = 32
ROT_DIM = DA_QK // 4
ROPE_THETA = 500000.0
SB_W = SB_HEADS * SB_DIM
GDN_KW = GDN_HEADS * GDN_DK
GDN_W = GDN_HEADS * GDN_DV
DA_W = DA_HEADS * DA_V
CONV_K = 4
N_EXPERTS = 32
TOP_K = 4
D_FF = D_MODEL
SWIGLU_ALPHA = 1.702
SWIGLU_LIMIT = 7.0
MOE_BLOCK = 256
LN_EPS = 1e-5
RMS_EPS = 1e-6
NEG_BIG = -1e30
LOG2E = math.log2(math.e)
DMA_UNROLL = 8

C_SB = 0
C_G4 = 3 * SB_W
C_DQK = C_G4 + 4 * GDN_W
C_DV = C_DQK + 2 * DA_W
W_MAIN = C_DV + DA_W


def _dot(a, b):
    return jnp.dot(a, b, preferred_element_type=F32)


def _dot_nt(a, b):
    return lax.dot_general(a, b, (((1,), (1,)), ((), ())), preferred_element_type=F32)


def _dot_tn(a, b):
    return lax.dot_general(a, b, (((0,), (0,)), ((), ())), preferred_element_type=F32)


def _split3(x):
    h1 = x.astype(BF16)
    r1 = x - h1.astype(F32)
    h2 = r1.astype(BF16)
    h3 = (r1 - h2.astype(F32)).astype(BF16)
    return h1, h2, h3


def _mm(a, b, passes):
    ah = a.astype(BF16)
    bh = b.astype(BF16)
    if passes == 1:
        return _dot(ah, bh)
    al = (a - ah.astype(F32)).astype(BF16)
    bl = (b - bh.astype(F32)).astype(BF16)
    return _dot(ah, bh) + (_dot(al, bh) + _dot(ah, bl))


def _silu(x):
    return x * jax.nn.sigmoid(x)


def _softplus(x):
    return jnp.maximum(x, 0.0) + jnp.log(1.0 + jnp.exp(-jnp.abs(x)))


def _cparams(sem):
    return pltpu.CompilerParams(dimension_semantics=sem, vmem_limit_bytes=VMEM_LIMIT)


def _ada_kernel(c_ref, w_ref, b_ref, o_ref):
    cond = _silu(c_ref[...])
    w = w_ref[0]
    c1, c2, c3 = _split3(cond)
    w1, w2, w3 = _split3(w)
    acc = _dot(c1, w1) + (_dot(c1, w2) + _dot(c2, w1)) + (_dot(c2, w2) + _dot(c1, w3) + _dot(c3, w1))
    o_ref[0] = acc + b_ref[0]


def _ada_call(c, w_ada, b_ada):
    depth, d, d6 = w_ada.shape
    b = c.shape[0]
    nj = d6 // d
    return pl.pallas_call(
        _ada_kernel,
        out_shape=jax.ShapeDtypeStruct((depth, b, d6), F32),
        grid=(depth, nj),
        in_specs=[
            pl.BlockSpec((b, d), lambda l, j: (0, 0)),
            pl.BlockSpec((1, d, d), lambda l, j: (l, 0, j)),
            pl.BlockSpec((1, 1, d), lambda l, j: (l, 0, j)),
        ],
        out_specs=pl.BlockSpec((1, b, d), lambda l, j: (l, 0, j)),
        compiler_params=_cparams(("parallel", "parallel")),
        name="ada_mod",
    )(c, w_ada, b_ada.reshape(depth, 1, d6))


def _inproj_kernel(x_ref, sc_ref, sh_ref, w_ref, wab_ref, wabt_ref, cos_ref, sinm_ref, sinp_ref,
                   sbq_ref, sbk_ref, sbv_ref, g4_ref, daq_ref, dak_ref, dav_ref, gab_ref, gabt_ref):
    h = (x_ref[...] * (1.0 + sc_ref[0]) + sh_ref[0]).astype(BF16)

    def seg(a, n):
        return _dot(h, w_ref[:, a:a + n])

    sbq_ref[...] = (seg(C_SB, SB_W) * (SB_DIM ** -0.5 * LOG2E)).astype(BF16)
    sbk_ref[...] = seg(C_SB + SB_W, SB_W).astype(BF16)
    sbv_ref[...] = seg(C_SB + 2 * SB_W, SB_W).astype(BF16)
    for j in range(4):
        g4_ref[:, j * GDN_W:(j + 1) * GDN_W] = seg(C_G4 + j * GDN_W, GDN_W)
    cos = cos_ref[...]
    sinm = sinm_ref[...]
    sinp = sinp_ref[...]
    for ref, base, scale in ((daq_ref, C_DQK, DA_QK ** -0.5 * LOG2E), (dak_ref, C_DQK + DA_W, 1.0)):
        for j in range(DA_W // LANES):
            a = seg(base + j * LANES, LANES)
            r = a * cos + pltpu.roll(a, LANES - ROT_DIM // 2, 1) * sinm + pltpu.roll(a, ROT_DIM // 2, 1) * sinp
            ref[:, j * LANES:(j + 1) * LANES] = (r * scale).astype(BF16)
    dav_ref[...] = seg(C_DV, DA_W).astype(BF16)
    gab_ref[...] = _dot(h, wab_ref[...])
    gabt_ref[...] = _dot_nt(wabt_ref[...], h)


def _inproj_call(x2, sc, sh, w_main, w_ab, w_abt, cos_t, sinm_t, sinp_t, *, seq, tm):
    n, d = x2.shape
    nblk_s = seq // tm
    row = lambda i: (i, 0)
    bat = lambda i: (i // nblk_s, 0, 0)
    pos = lambda i: (i % nblk_s, 0)
    const = lambda i: (0, 0)
    outs = [
        (SB_W, BF16), (SB_W, BF16), (SB_W, BF16), (4 * GDN_W, F32),
        (DA_W, BF16), (DA_W, BF16), (DA_W, BF16), (LANES, F32),
    ]
    out_shape = [jax.ShapeDtypeStruct((n, w), dt) for w, dt in outs] + [jax.ShapeDtypeStruct((8, n), F32)]
    out_specs = [pl.BlockSpec((tm, w), row) for w, _ in outs] + [pl.BlockSpec((8, tm), lambda i: (0, i))]
    return pl.pallas_call(
        _inproj_kernel,
        out_shape=out_shape,
        grid=(n // tm,),
        in_specs=[
            pl.BlockSpec((tm, d), row),
            pl.BlockSpec((1, 1, d), bat),
            pl.BlockSpec((1, 1, d), bat),
            pl.BlockSpec(w_main.shape, const),
            pl.BlockSpec(w_ab.shape, const),
            pl.BlockSpec(w_abt.shape, const),
            pl.BlockSpec((tm, LANES), pos),
            pl.BlockSpec((tm, LANES), pos),
            pl.BlockSpec((tm, LANES), pos),
        ],
        out_specs=out_specs,
        compiler_params=_cparams(("parallel",)),
        name="inproj",
    )(x2, sc, sh, w_main, w_ab, w_abt, cos_t, sinm_t, sinp_t)


ATT_RC = 32


def _att_pipeline(i, stage_a, stage_b, stage_c):
    stage_a(0, i)
    stage_a(1, jnp.maximum(i - 1, 0))
    stage_b(0, 0, True)
    stage_c(0, i, True)
    stage_b(1, 1, False)
    stage_a(0, jnp.maximum(i - 2, 0))

    def step(parity, j):
        stage_c(1 - parity, j, False)
        stage_b(parity, parity, False)
        stage_a(1 - parity, jnp.maximum(j - 2, 0))

    def body(pp, _):
        j = i - 1 - 2 * pp
        step(0, j)
        step(1, j - 1)
        return 0

    lax.fori_loop(0, (i + 1) // 2, body, 0)


def _sb_kernel(q_ref, k_ref, v_ref, o_ref, z0, z1, t0, t1, b0, b1, tot0, tot1, hi0, hi1, lo0, lo1, w0, w1, carry_scr,
               acc_scr, *, tq):
    z_scr, t_scr, b_scr, tot_scr = (z0, z1), (t0, t1), (b0, b1), (tot0, tot1)
    hi_scr, lo_scr, w_scr = (hi0, hi1), (lo0, lo1), (w0, w1)
    i = pl.program_id(2)
    q = q_ref[...]
    lane = lax.broadcasted_iota(I32, (1, LANES), 1)
    rows = lax.broadcasted_iota(I32, (tq, tq), 0)
    cols = lax.broadcasted_iota(I32, (tq, tq), 1)
    u = jnp.where(cols < rows, 1.0, 0.0).astype(BF16)
    qhs = [jnp.where((lane >= SB_DIM * hh) & (lane < SB_DIM * (hh + 1)), q, jnp.zeros_like(q)) for hh in range(2)]
    crow = lax.broadcasted_iota(I32, (ATT_RC, tq), 0)
    ccol = lax.broadcasted_iota(I32, (ATT_RC, tq), 1)
    chunks = [slice(r * ATT_RC, (r + 1) * ATT_RC) for r in range(tq // ATT_RC)]
    carry_scr[...] = jnp.zeros_like(carry_scr)
    acc_scr[...] = jnp.zeros_like(acc_scr)

    def stage_a(zslot, j):
        kb = k_ref[pl.ds(pl.multiple_of(j * tq, tq), tq), :]
        for hh in range(2):
            z_scr[zslot][hh] = _dot_nt(qhs[hh], kb)

    def stage_b(zslot, tslot, masked):
        for hh in range(2):
            for r, rs in enumerate(chunks):
                z = z_scr[zslot][hh, rs, :]
                sp = jnp.maximum(z, 0.0) + jnp.log2(1.0 + jnp.exp2(-jnp.abs(z)))
                lk = -sp
                if masked:
                    lk = jnp.where(ccol < crow + r * ATT_RC, lk, 0.0)
                hi = lk.astype(BF16)
                hi_scr[tslot][hh, rs, :] = hi
                lo_scr[tslot][hh, rs, :] = (lk - hi.astype(F32)).astype(BF16)
                t_scr[tslot][hh, rs, :] = z - sp
                tot_scr[tslot][hh, rs, :] = jnp.broadcast_to(jnp.sum(lk, axis=1, keepdims=True), (ATT_RC, LANES))
        for hh in range(2):
            b_scr[tslot][hh] = _dot(hi_scr[tslot][hh], u) + _dot(lo_scr[tslot][hh], u)

    def stage_c(tslot, j, masked):
        vb = v_ref[pl.ds(pl.multiple_of(jnp.maximum(j, 0) * tq, tq), tq), :]
        vb = jnp.where(j >= 0, vb, jnp.zeros_like(vb))
        for hh in range(2):
            for r, rs in enumerate(chunks):
                carry = carry_scr[hh, rs, :]
                w = jnp.exp2(t_scr[tslot][hh, rs, :] + b_scr[tslot][hh, rs, :]
                             + jnp.concatenate([carry] * (tq // LANES), axis=1))
                if masked:
                    w = jnp.where(ccol < crow + r * ATT_RC, w, 0.0)
                w_scr[tslot][hh, rs, :] = w.astype(BF16)
                carry_scr[hh, rs, :] = carry + tot_scr[tslot][hh, rs, :]
        for hh in range(2):
            acc_scr[hh] += _dot(w_scr[tslot][hh], vb)

    _att_pipeline(i, stage_a, stage_b, stage_c)
    o_ref[...] = jnp.where(lane < SB_DIM, acc_scr[0], acc_scr[1]).astype(BF16)


def _sb_call(q, k, v, *, batch, seq, tq):
    n = q.shape[0]
    nq = seq // tq
    return pl.pallas_call(
        functools.partial(_sb_kernel, tq=tq),
        out_shape=jax.ShapeDtypeStruct((n, SB_W), BF16),
        grid=(batch, SB_W // LANES, nq),
        in_specs=[
            pl.BlockSpec((tq, LANES), lambda b, p, i: (b * nq + i, p)),
            pl.BlockSpec((seq, LANES), lambda b, p, i: (b, p)),
            pl.BlockSpec((seq, LANES), lambda b, p, i: (b, p)),
        ],
        out_specs=pl.BlockSpec((tq, LANES), lambda b, p, i: (b * nq + i, p)),
        scratch_shapes=[pltpu.VMEM((2, tq, tq), F32)] * 6 + [
            pltpu.VMEM((2, tq, LANES), F32),
            pltpu.VMEM((2, tq, LANES), F32),
        ] + [pltpu.VMEM((2, tq, tq), BF16)] * 6 + [
            pltpu.VMEM((2, tq, LANES), F32),
            pltpu.VMEM((2, tq, LANES), F32),
        ],
        compiler_params=_cparams(("parallel", "parallel", "arbitrary")),
        name="sb_attn",
    )(q, k, v)


def _da_kernel(lam_ref, q_ref, k_ref, v_ref, subw_ref, o_ref, s0, s1, p0, p1, al0, al1, m_scr, acc_scr,
               *, tq, lambda_init):
    s_scr, p_scr, al_scr = (s0, s1), (p0, p1), (al0, al1)
    i = pl.program_id(2)
    lp = lam_ref[...]
    lam = (jnp.exp(jnp.sum(lp[0:1] * lp[1:2], axis=1, keepdims=True))
           - jnp.exp(jnp.sum(lp[2:3] * lp[3:4], axis=1, keepdims=True)) + lambda_init)
    q = q_ref[...]
    lane = lax.broadcasted_iota(I32, (1, LANES), 1)
    first = lane < DA_V
    n_maps = LANES // DA_QK
    qms = [jnp.where((lane >= DA_QK * g) & (lane < DA_QK * (g + 1)), q, jnp.zeros_like(q)) for g in range(n_maps)]
    crow = lax.broadcasted_iota(I32, (ATT_RC, tq), 0)
    ccol = lax.broadcasted_iota(I32, (ATT_RC, tq), 1)
    chunks = [slice(r * ATT_RC, (r + 1) * ATT_RC) for r in range(tq // ATT_RC)]
    m_scr[...] = jnp.full(m_scr.shape, NEG_BIG, F32)
    acc_scr[...] = jnp.zeros_like(acc_scr)

    def stage_a(sslot, j):
        kb = k_ref[pl.ds(pl.multiple_of(j * tq, tq), tq), :]
        for g in range(n_maps):
            s_scr[sslot][g] = _dot_nt(qms[g], kb)

    def stage_b(sslot, pslot, masked):
        for g in range(n_maps):
            for r, rs in enumerate(chunks):
                s = s_scr[sslot][g, rs, :]
                if masked:
                    s = jnp.where(ccol <= crow + r * ATT_RC, s, NEG_BIG)
                m_old = m_scr[g, rs, :]
                m_new = jnp.maximum(m_old, jnp.broadcast_to(jnp.max(s, axis=1, keepdims=True), (ATT_RC, LANES)))
                al_scr[pslot][g, rs, :] = jnp.exp2(m_old - m_new)
                p_scr[pslot][g, rs, :] = jnp.exp2(s - jnp.concatenate([m_new] * (tq // LANES), axis=1)).astype(BF16)
                m_scr[g, rs, :] = m_new

    def stage_c(pslot, j, masked):
        del masked
        vb = v_ref[pl.ds(pl.multiple_of(jnp.maximum(j, 0) * tq, tq), tq), :]
        one = jnp.where(j >= 0, jnp.ones_like(vb), jnp.zeros_like(vb))
        vb = jnp.where(j >= 0, vb, jnp.zeros_like(vb))
        vh = (jnp.where(first, vb, one), jnp.where(first, one, vb))
        for g in range(n_maps):
            acc_scr[g] = al_scr[pslot][g] * acc_scr[g] + _dot(p_scr[pslot][g], vh[g // 2])

    _att_pipeline(i, stage_a, stage_b, stage_c)
    maps = [acc_scr[g] / pltpu.roll(acc_scr[g], DA_V, 1) for g in range(n_maps)]
    o = jnp.where(first, maps[0] - lam * maps[1], maps[2] - lam * maps[3])
    sq = o * o
    ms0 = jnp.sum(jnp.where(first, sq, 0.0), axis=1, keepdims=True) * (1.0 / DA_V)
    ms1 = jnp.sum(jnp.where(first, 0.0, sq), axis=1, keepdims=True) * (1.0 / DA_V)
    ms = jnp.where(first, ms0, ms1)
    o_ref[...] = ((o * lax.rsqrt(ms + RMS_EPS) * subw_ref[...]) * (1.0 - lambda_init)).astype(BF16)


def _da_call(lam_p, q, k, v, subw2, *, batch, seq, tq, lambda_init):
    n = q.shape[0]
    nq = seq // tq
    n_maps = LANES // DA_QK
    return pl.pallas_call(
        functools.partial(_da_kernel, tq=tq, lambda_init=lambda_init),
        out_shape=jax.ShapeDtypeStruct((n, DA_W), BF16),
        grid=(batch, DA_W // LANES, nq),
        in_specs=[
            pl.BlockSpec(lam_p.shape, lambda b, p, i: (0, 0)),
            pl.BlockSpec((tq, LANES), lambda b, p, i: (b * nq + i, p)),
            pl.BlockSpec((seq, LANES), lambda b, p, i: (b, p)),
            pl.BlockSpec((seq, LANES), lambda b, p, i: (b, p)),
            pl.BlockSpec((1, LANES), lambda b, p, i: (0, 0)),
        ],
        out_specs=pl.BlockSpec((tq, LANES), lambda b, p, i: (b * nq + i, p)),
        scratch_shapes=[
            pltpu.VMEM((n_maps, tq, tq), F32),
            pltpu.VMEM((n_maps, tq, tq), F32),
            pltpu.VMEM((n_maps, tq, tq), BF16),
            pltpu.VMEM((n_maps, tq, tq), BF16),
            pltpu.VMEM((n_maps, tq, LANES), F32),
            pltpu.VMEM((n_maps, tq, LANES), F32),
            pltpu.VMEM((n_maps, tq, LANES), F32),
            pltpu.VMEM((n_maps, tq, LANES), F32),
        ],
        compiler_params=_cparams(("parallel", "parallel", "arbitrary")),
        name="da_attn",
    )(lam_p, q, k, v, subw2)


GDN_CHUNK = 128
CONV_PAD = 8
GDN_TT = 512


def _gdn_kernel(q_ref, k_ref, v_ref, z_ref, ab_ref, abt_ref, cwq_ref, cwk_ref, cwv_ref, prow_ref, pcol_ref,
                nw_ref, o_ref, state_ref, cq_ref, ck_ref, cv_ref, *, tt):
    hd = pl.program_id(1)
    i = pl.program_id(2)
    c = GDN_CHUNK
    lane = lax.broadcasted_iota(I32, (1, LANES), 1)
    sub8 = lax.broadcasted_iota(I32, (8, 1), 0)
    rows = lax.broadcasted_iota(I32, (c, c), 0)
    cols = lax.broadcasted_iota(I32, (c, c), 1)
    lower = cols <= rows
    strict = cols < rows
    eye = jnp.where(cols == rows, 1.0, 0.0)
    ltri = jnp.where(lower, 1.0, 0.0).astype(BF16)
    utri = jnp.where(rows <= cols, 1.0, 0.0).astype(BF16)

    @pl.when(i == 0)
    def _():
        state_ref[...] = jnp.zeros_like(state_ref)
        for r in (cq_ref, ck_ref, cv_ref):
            r[0:CONV_PAD, :] = jnp.zeros((CONV_PAD, LANES), F32)

    @pl.when(i > 0)
    def _():
        for r in (cq_ref, ck_ref, cv_ref):
            r[0:CONV_PAD, :] = r[tt:tt + CONV_PAD, :]

    def conv_silu(x_ref, buf_ref, w_ref):
        buf_ref[CONV_PAD:CONV_PAD + tt, :] = x_ref[...]
        acc = None
        for j in range(CONV_K):
            off = CONV_PAD - (CONV_K - 1) + j
            term = buf_ref[off:off + tt, :] * w_ref[j:j + 1, :]
            acc = term if acc is None else acc + term
        return _silu(acc)

    qc = conv_silu(q_ref, cq_ref, cwq_ref)
    kc = conv_silu(k_ref, ck_ref, cwk_ref)
    vc = conv_silu(v_ref, cv_ref, cwv_ref)
    qn = qc * lax.rsqrt(jnp.sum(qc * qc, axis=1, keepdims=True) + 1e-6) * (GDN_DK ** -0.5)
    kn = kc * lax.rsqrt(jnp.sum(kc * kc, axis=1, keepdims=True) + 1e-6)

    ab = ab_ref[...]
    g_all = -jnp.exp(prow_ref[0:1, :]) * _softplus(ab + prow_ref[1:2, :])
    g_col = jnp.sum(jnp.where(lane == hd, g_all, 0.0), axis=1, keepdims=True)
    beta_col = jnp.sum(jnp.where(lane == hd + GDN_HEADS, jax.nn.sigmoid(ab), 0.0), axis=1, keepdims=True)
    abt = abt_ref[...]
    g_allt = -jnp.exp(pcol_ref[:, 0:1]) * _softplus(abt + pcol_ref[:, 1:2])
    g_row = jnp.sum(jnp.where(sub8 == hd, g_allt, 0.0), axis=0, keepdims=True)

    nc = tt // c
    sls = [slice(ci * c, (ci + 1) * c) for ci in range(nc)]
    gcs, decays, kbs, ms, attns = [], [], [], [], []
    for sl in sls:
        g1, g2, g3 = _split3(jnp.broadcast_to(g_col[sl], (c, LANES)))
        gc = _dot(ltri, g1) + _dot(ltri, g2) + _dot(ltri, g3)
        r1, r2, r3 = _split3(jnp.broadcast_to(g_row[:, sl], (8, c)))
        gcr = (_dot(r1, utri) + _dot(r2, utri) + _dot(r3, utri))[0:1, :]
        gcs.append(gc)
        decays.append(jnp.where(lower, jnp.exp(jnp.where(lower, gc - gcr, 0.0)), 0.0))
    for sl, decay in zip(sls, decays):
        kb = kn[sl] * beta_col[sl]
        kt = kn[sl].T
        kbs.append(kb)
        ms.append(jnp.where(strict, _mm(kb, kt, 1) * decay, 0.0))
        attns.append(jnp.where(lower, _mm(qn[sl], kt, 1) * decay, 0.0))
    ps = [-m for m in ms]
    tinvs = [eye + p for p in ps]
    for _ in range(int(math.log2(c)) - 2):
        ps = [_mm(p, p, 1) for p in ps]
        tinvs = [t + _mm(t, p, 1) for t, p in zip(tinvs, ps)]
    resids = [eye - _mm(eye + m, t, 3) for m, t in zip(ms, tinvs)]
    tinvs = [t + _mm(t, r, 1) for t, r in zip(tinvs, resids)]
    egs = [jnp.exp(gc) for gc in gcs]
    sols = [_mm(t, jnp.concatenate([vc[sl] * beta_col[sl], kb * eg], axis=1), 1)
            for t, sl, kb, eg in zip(tinvs, sls, kbs, egs)]
    for ci, sl in enumerate(sls):
        gc, eg = gcs[ci], egs[ci]
        u_c = sols[ci][:, :GDN_DV]
        w_c = sols[ci][:, GDN_DV:]
        st = state_ref[...]
        v_new = u_c - _mm(w_c, st, 1)
        o = _mm(qn[sl] * eg, st, 1) + _mm(attns[ci], v_new, 1)
        g_last = gc[c - 1:c, :]
        kd = kn[sl] * jnp.exp(g_last - gc)
        state_ref[...] = st * jnp.exp(g_last) + _mm(kd.T, v_new, 1)
        on = o * lax.rsqrt(jnp.mean(o * o, axis=1, keepdims=True) + RMS_EPS) * nw_ref[...]
        o_ref[sl, :] = (on * _silu(z_ref[sl, :])).astype(BF16)


def _gdn_call(g4, gab, gabt, conv_w, prow, pcol, nw2, *, batch, seq, tt):
    n = g4.shape[0]
    ns = seq // tt
    tok = lambda off: (lambda b, h, i: (b * ns + i, off + h))
    cw = lambda off: (lambda b, h, i: (0, off + h))
    const = lambda b, h, i: (0, 0)
    return pl.pallas_call(
        functools.partial(_gdn_kernel, tt=tt),
        out_shape=jax.ShapeDtypeStruct((n, GDN_W), BF16),
        grid=(batch, GDN_HEADS, ns),
        in_specs=[
            pl.BlockSpec((tt, LANES), tok(0)),
            pl.BlockSpec((tt, LANES), tok(GDN_HEADS)),
            pl.BlockSpec((tt, LANES), tok(2 * GDN_HEADS)),
            pl.BlockSpec((tt, LANES), tok(3 * GDN_HEADS)),
            pl.BlockSpec((tt, LANES), lambda b, h, i: (b * ns + i, 0)),
            pl.BlockSpec((8, tt), lambda b, h, i: (0, b * ns + i)),
            pl.BlockSpec((CONV_K, LANES), cw(0)),
            pl.BlockSpec((CONV_K, LANES), cw(GDN_HEADS)),
            pl.BlockSpec((CONV_K, LANES), cw(2 * GDN_HEADS)),
            pl.BlockSpec(prow.shape, const),
            pl.BlockSpec(pcol.shape, const),
            pl.BlockSpec((1, LANES), const),
        ],
        out_specs=pl.BlockSpec((tt, LANES), tok(0)),
        scratch_shapes=[
            pltpu.VMEM((GDN_DK, GDN_DV), F32),
            pltpu.VMEM((tt + CONV_PAD, LANES), F32),
            pltpu.VMEM((tt + CONV_PAD, LANES), F32),
            pltpu.VMEM((tt + CONV_PAD, LANES), F32),
        ],
        compiler_params=_cparams(("parallel", "parallel", "arbitrary")),
        name="gdn",
    )(g4, g4, g4, g4, gab, gabt, conv_w, conv_w, conv_w, prow, pcol, nw2)


def _layer_norm(y, g, b):
    mu = jnp.mean(y, axis=1, keepdims=True)
    yc = y - mu
    var = jnp.mean(yc * yc, axis=1, keepdims=True)
    return yc * lax.rsqrt(var + LN_EPS) * g + b


def _outproj_kernel(x_ref, osb_ref, ogdn_ref, oda_ref, w_ref, g1_ref, lng_ref, lnb_ref, sc2_ref, sh2_ref,
                    wr_ref, br_ref, x1_ref, h2_ref, route_ref, *, dn_alpha):
    mix = (_dot(osb_ref[...], w_ref[0:SB_W, :]) + _dot(ogdn_ref[...], w_ref[SB_W:SB_W + GDN_W, :])
           + _dot(oda_ref[...], w_ref[SB_W + GDN_W:, :]))
    y = dn_alpha * x_ref[...] + (1.0 + g1_ref[0]) * mix
    x1 = _layer_norm(y, lng_ref[...], lnb_ref[...])
    x1_ref[...] = x1
    h2 = x1 * (1.0 + sc2_ref[0]) + sh2_ref[0]
    h2_ref[...] = h2.reshape(h2_ref.shape)
    lane = lax.broadcasted_iota(I32, (1, LANES), 1)
    logits = _dot(h2.astype(BF16), wr_ref[...]) + br_ref[...]
    logits = jnp.where(lane < N_EXPERTS, logits, NEG_BIG)
    vals, idxs = [], []
    for _ in range(TOP_K):
        mx = jnp.max(logits, axis=1, keepdims=True)
        ix = jnp.min(jnp.where(logits == mx, lane, LANES), axis=1, keepdims=True)
        vals.append(mx)
        idxs.append(ix)
        logits = jnp.where(lane == ix, NEG_BIG, logits)
    es = [jnp.exp(v - vals[0]) for v in vals]
    den = es[0] + es[1] + es[2] + es[3]
    route = jnp.zeros(route_ref.shape, F32)
    for kk in range(TOP_K):
        route = jnp.where(lane == kk, es[kk] / den, route)
        route = jnp.where(lane == TOP_K + kk, idxs[kk].astype(F32), route)
    route_ref[...] = route


def _outproj_call(x2, osb, ogdn, oda, w_out, g1, lng, lnb, sc2, sh2, wr, br, *, seq, tm, dn_alpha):
    n, d = x2.shape
    nblk_s = seq // tm
    row = lambda i: (i, 0)
    bat = lambda i: (i // nblk_s, 0, 0)
    const = lambda i: (0, 0)
    return pl.pallas_call(
        functools.partial(_outproj_kernel, dn_alpha=dn_alpha),
        out_shape=[jax.ShapeDtypeStruct((n, d), F32), jax.ShapeDtypeStruct((n, d // LANES, LANES), F32),
                   jax.ShapeDtypeStruct((n, LANES), F32)],
        grid=(n // tm,),
        in_specs=[
            pl.BlockSpec((tm, d), row),
            pl.BlockSpec((tm, SB_W), row),
            pl.BlockSpec((tm, GDN_W), row),
            pl.BlockSpec((tm, DA_W), row),
            pl.BlockSpec(w_out.shape, const),
            pl.BlockSpec((1, 1, d), bat),
            pl.BlockSpec((1, d), const),
            pl.BlockSpec((1, d), const),
            pl.BlockSpec((1, 1, d), bat),
            pl.BlockSpec((1, 1, d), bat),
            pl.BlockSpec(wr.shape, const),
            pl.BlockSpec((1, LANES), const),
        ],
        out_specs=[pl.BlockSpec((tm, d), row), pl.BlockSpec((tm, d // LANES, LANES), lambda i: (i, 0, 0)),
                   pl.BlockSpec((tm, LANES), row)],
        compiler_params=_cparams(("parallel",)),
        name="outproj_ln_router",
    )(x2, osb, ogdn, oda, w_out, g1, lng, lnb, sc2, sh2, wr, br)


PREP_ROWS = 256
PREP_COLS = 512


def _wprep_kernel(wgu_ref, wd_ref, ogu_ref, od_ref):
    rows = lax.broadcasted_iota(I32, (PREP_COLS, PREP_COLS // 2), 0)
    cols = lax.broadcasted_iota(I32, (PREP_COLS, PREP_COLS // 2), 1)
    sel_even = jnp.where(rows == 2 * cols, 1.0, 0.0).astype(BF16)
    sel_odd = jnp.where(rows == 2 * cols + 1, 1.0, 0.0).astype(BF16)
    half = PREP_COLS // 2
    for j in range(2 * D_FF // PREP_COLS):
        w = wgu_ref[0, 0, :, j * PREP_COLS:(j + 1) * PREP_COLS].astype(BF16)
        ogu_ref[0, 0, :, j * half:(j + 1) * half] = _dot(w, sel_even).astype(BF16)
        ogu_ref[0, 0, :, D_FF + j * half:D_FF + (j + 1) * half] = _dot(w, sel_odd).astype(BF16)
    od_ref[...] = wd_ref[...].astype(BF16)


def _wprep_call(w_gu, w_down):
    depth, ne, d, f2 = w_gu.shape
    nr = d // PREP_ROWS
    return pl.pallas_call(
        _wprep_kernel,
        out_shape=[jax.ShapeDtypeStruct(w_gu.shape, BF16), jax.ShapeDtypeStruct(w_down.shape, BF16)],
        grid=(depth, ne, nr),
        in_specs=[
            pl.BlockSpec((1, 1, PREP_ROWS, f2), lambda l, e, r: (l, e, r, 0)),
            pl.BlockSpec((1, 1, PREP_ROWS, d), lambda l, e, r: (l, e, r, 0)),
        ],
        out_specs=[
            pl.BlockSpec((1, 1, PREP_ROWS, f2), lambda l, e, r: (l, e, r, 0)),
            pl.BlockSpec((1, 1, PREP_ROWS, d), lambda l, e, r: (l, e, r, 0)),
        ],
        compiler_params=_cparams(("parallel", "parallel", "parallel")),
        name="expert_weight_prep",
    )(w_gu, w_down)


def _moe_kernel(bexp_ref, nvalid_ref, tokc_ref, tokn_ref, h_hbm, wgu_ref, bgu_ref, wd_ref, bd_ref, y_ref, xbuf0, xbuf1,
                sem, *, n_tok):
    i = pl.program_id(0)
    xbufs = (xbuf0, xbuf1)
    valid = nvalid_ref[i] > 0

    def start_row(tok_ref, r, slot):
        t = jnp.minimum(tok_ref[0, 0, r], n_tok - 1)
        pltpu.make_async_copy(h_hbm.at[pl.ds(t, 1)], xbufs[slot].at[pl.ds(r, 1)], sem.at[slot]).start()

    def wait_block(slot):
        pltpu.make_async_copy(h_hbm.at[pl.ds(0, MOE_BLOCK)], xbufs[slot], sem.at[slot]).wait()

    @pl.when((i == 0) & valid)
    def _():
        lax.fori_loop(0, MOE_BLOCK, lambda r, c: (start_row(tokc_ref, r, 0), c)[1], 0, unroll=DMA_UNROLL)

    for slot in range(2):
        @pl.when(valid & (i % 2 == slot))
        def _(slot=slot):
            wait_block(slot)
            xb = xbufs[slot][...].reshape(MOE_BLOCK, D_MODEL).astype(BF16)
            hgu = _dot(xb, wgu_ref[0, 0]) + bgu_ref[0]
            for r in range(MOE_BLOCK):
                start_row(tokn_ref, r, 1 - slot)
            glu = jnp.minimum(hgu[:, :D_FF], SWIGLU_LIMIT)
            lin = jnp.clip(hgu[:, D_FF:], -SWIGLU_LIMIT, SWIGLU_LIMIT)
            act = glu * jax.nn.sigmoid(SWIGLU_ALPHA * glu) * (lin + 1.0)
            y_ref[...] = (_dot(act.astype(BF16), wd_ref[0, 0]) + bd_ref[0]).reshape(y_ref.shape)

    @pl.when(jnp.logical_not(valid))
    def _():
        y_ref[...] = jnp.zeros_like(y_ref)

    for slot in range(2):
        @pl.when(jnp.logical_not(valid) & (i > 0) & (nvalid_ref[jnp.maximum(i - 1, 0)] > 0) & (i % 2 == slot))
        def _(slot=slot):
            wait_block(slot)


def _moe_call(block_expert, nvalid, row_tok3, h2, wgu_all, bgu, wd_all, bd, *, layer):
    n_blocks = block_expert.shape[0]
    n = h2.shape[0]
    d = D_MODEL
    row_tile = (d // LANES, LANES)
    wsel = lambda i, be, nv: (layer, be[i], 0, 0)
    bsel = lambda i, be, nv: (be[i], 0, 0)
    return pl.pallas_call(
        functools.partial(_moe_kernel, n_tok=n),
        out_shape=jax.ShapeDtypeStruct((n_blocks * MOE_BLOCK,) + row_tile, F32),
        grid_spec=pltpu.PrefetchScalarGridSpec(
            num_scalar_prefetch=2,
            grid=(n_blocks,),
            in_specs=[
                pl.BlockSpec((1, 1, MOE_BLOCK), lambda i, be, nv: (i, 0, 0), memory_space=pltpu.SMEM),
                pl.BlockSpec((1, 1, MOE_BLOCK), lambda i, be, nv: (jnp.minimum(i + 1, n_blocks - 1), 0, 0),
                             memory_space=pltpu.SMEM),
                pl.BlockSpec(memory_space=pl.ANY),
                pl.BlockSpec((1, 1, d, 2 * D_FF), wsel),
                pl.BlockSpec((1, 1, 2 * D_FF), bsel),
                pl.BlockSpec((1, 1, D_FF, d), wsel),
                pl.BlockSpec((1, 1, d), bsel),
            ],
            out_specs=pl.BlockSpec((MOE_BLOCK,) + row_tile, lambda i, be, nv: (i, 0, 0)),
            scratch_shapes=[pltpu.VMEM((MOE_BLOCK,) + row_tile, F32), pltpu.VMEM((MOE_BLOCK,) + row_tile, F32),
                            pltpu.SemaphoreType.DMA((2,))],
        ),
        compiler_params=_cparams(("arbitrary",)),
        name="moe_experts",
    )(block_expert, nvalid, row_tok3, row_tok3, h2, wgu_all, bgu, wd_all, bd)


def _combine_kernel(posc_ref, posn_ref, y_hbm, x1_ref, route_ref, g2_ref, lng_ref, lnb_ref, o_ref, ybuf, sem,
                    *, tm, dn_alpha):
    i = pl.program_id(0)
    nb = pl.num_programs(0)
    slot = i % 2

    def issue(pos_ref, slot_):
        for kk in range(TOP_K):
            def body(r, carry, kk=kk):
                p = pos_ref[0, 0, kk * tm + r]
                pltpu.make_async_copy(y_hbm.at[pl.ds(p, 1)], ybuf.at[slot_, kk, pl.ds(r, 1)], sem.at[slot_]).start()
                return carry
            lax.fori_loop(0, tm, body, 0, unroll=DMA_UNROLL)

    @pl.when(i == 0)
    def _():
        issue(posc_ref, 0)

    @pl.when(i + 1 < nb)
    def _():
        issue(posn_ref, 1 - slot)

    for kk in range(TOP_K):
        pltpu.make_async_copy(y_hbm.at[pl.ds(0, tm)], ybuf.at[slot, kk], sem.at[slot]).wait()
    lane = lax.broadcasted_iota(I32, (1, LANES), 1)
    route = route_ref[...]
    ffn = None
    for kk in range(TOP_K):
        gate = jnp.sum(jnp.where(lane == kk, route, 0.0), axis=1, keepdims=True)
        term = gate * ybuf[slot, kk].reshape(tm, D_MODEL)
        ffn = term if ffn is None else ffn + term
    y = dn_alpha * x1_ref[...] + (1.0 + g2_ref[0]) * ffn
    o_ref[...] = _layer_norm(y, lng_ref[...], lnb_ref[...])


def _combine_call(pos3, y_rows, x1, route, g2, lng, lnb, *, seq, tm, dn_alpha):
    n, d = x1.shape
    nblk_s = seq // tm
    nblk = n // tm
    return pl.pallas_call(
        functools.partial(_combine_kernel, tm=tm, dn_alpha=dn_alpha),
        out_shape=jax.ShapeDtypeStruct((n, d), F32),
        grid=(nblk,),
        in_specs=[
            pl.BlockSpec((1, 1, TOP_K * tm), lambda i: (i, 0, 0), memory_space=pltpu.SMEM),
            pl.BlockSpec((1, 1, TOP_K * tm), lambda i: (jnp.minimum(i + 1, nblk - 1), 0, 0),
                         memory_space=pltpu.SMEM),
            pl.BlockSpec(memory_space=pl.ANY),
            pl.BlockSpec((tm, d), lambda i: (i, 0)),
            pl.BlockSpec((tm, LANES), lambda i: (i, 0)),
            pl.BlockSpec((1, 1, d), lambda i: (i // nblk_s, 0, 0)),
            pl.BlockSpec((1, d), lambda i: (0, 0)),
            pl.BlockSpec((1, d), lambda i: (0, 0)),
        ],
        out_specs=pl.BlockSpec((tm, d), lambda i: (i, 0)),
        scratch_shapes=[pltpu.VMEM((2, TOP_K, tm, d // LANES, LANES), F32), pltpu.SemaphoreType.DMA((2,))],
        compiler_params=_cparams(("arbitrary",)),
        name="moe_combine_ln",
    )(pos3, pos3, y_rows, x1, route, g2, lng, lnb)


def _routing(route, n):
    e_flat = route[:, TOP_K:2 * TOP_K].astype(I32).reshape(-1)
    na = n * TOP_K
    order = jnp.argsort(e_flat)
    inv = jnp.argsort(order)
    experts = jnp.arange(N_EXPERTS, dtype=I32)
    counts = jnp.sum((e_flat[:, None] == experts[None, :]).astype(I32), axis=0)
    starts = jnp.cumsum(counts) - counts
    padded = (counts + MOE_BLOCK - 1) // MOE_BLOCK * MOE_BLOCK
    padded_ends = jnp.cumsum(padded)
    padded_starts = padded_ends - padded
    n_rows = -(-na // MOE_BLOCK) * MOE_BLOCK + N_EXPERTS * MOE_BLOCK
    n_blocks = n_rows // MOE_BLOCK
    blk_start = jnp.arange(n_blocks, dtype=I32) * MOE_BLOCK
    block_expert = jnp.minimum(jnp.sum((padded_ends[None, :] <= blk_start[:, None]).astype(I32), axis=1),
                               N_EXPERTS - 1)
    nvalid = jnp.clip(padded_starts[block_expert] + counts[block_expert] - blk_start, 0, MOE_BLOCK)
    nvalid = jnp.where(blk_start < padded_ends[-1], nvalid, 0).astype(I32)
    row_e = jnp.repeat(block_expert, MOE_BLOCK)
    j = jnp.arange(n_rows, dtype=I32) - padded_starts[row_e]
    src = jnp.clip(starts[row_e] + j, 0, na - 1)
    row_tok = jnp.where(j < counts[row_e], order[src] // TOP_K, n).astype(I32)
    pos_flat = (padded_starts[e_flat] + inv - starts[e_flat]).astype(I32)
    return block_expert, nvalid, row_tok.reshape(n_blocks, 1, MOE_BLOCK), pos_flat.reshape(n, TOP_K)


def _rope_tables(seq):
    pos = jnp.arange(seq, dtype=F32)
    inv_freq = ROPE_THETA ** (-jnp.arange(0, ROT_DIM, 2, dtype=F32) / ROT_DIM)
    ang = pos[:, None] * inv_freq[None, :]
    cos, sin = jnp.cos(ang), jnp.sin(ang)
    half = ROT_DIM // 2
    ones = jnp.ones((seq, DA_QK - ROT_DIM), F32)
    zeros = jnp.zeros((seq, DA_QK - ROT_DIM), F32)
    zh = jnp.zeros((seq, half), F32)
    cos_g = jnp.concatenate([cos, cos, ones], axis=1)
    sinm_g = jnp.concatenate([-sin, zh, zeros], axis=1)
    sinp_g = jnp.concatenate([zh, sin, zeros], axis=1)
    rep = LANES // DA_QK
    return jnp.tile(cos_g, (1, rep)), jnp.tile(sinm_g, (1, rep)), jnp.tile(sinp_g, (1, rep))


def kernel(x, c, w_ada, b_ada, w_in, conv_w, gdn_a_log, gdn_dt_bias, gdn_norm_w, da_lambda, da_subln_w, w_out,
           ln1_g, ln1_b, w_router, b_router, w_gu, b_gu, w_down, b_down, ln2_g, ln2_b):
    batch, seq, d = x.shape
    depth = w_ada.shape[0]
    n = batch * seq
    dn_alpha = (2 * depth) ** 0.25
    tm = min(256, seq)
    tq = min(256, seq)
    tt = min(GDN_TT, seq)
    tc = min(128, seq)

    mod = _ada_call(c, w_ada, b_ada)
    wgu_all, wd_all = _wprep_call(w_gu, w_down)
    cos_t, sinm_t, sinp_t = _rope_tables(seq)
    x2 = x.reshape(n, d)
    for l in range(depth):
        lambda_init = 0.8 - 0.6 * math.exp(-0.3 * l)
        sh1, sc1, g1, sh2, sc2, g2 = [mod[l, :, j * d:(j + 1) * d].reshape(batch, 1, d) for j in range(6)]
        wl = w_in[l]
        p_sb = 3 * SB_W
        p_g = p_sb + 2 * GDN_KW + 2 * GDN_W
        p_ab = p_g + 2 * GDN_HEADS
        w_main = jnp.concatenate([wl[:, :p_g], wl[:, p_ab:]], axis=1).astype(BF16)
        w_ab = jnp.pad(wl[:, p_g:p_ab], ((0, 0), (0, LANES - 2 * GDN_HEADS))).astype(BF16)
        w_abt = wl[:, p_g:p_ab].T.astype(BF16)
        prow = jnp.zeros((8, LANES), F32).at[0, :GDN_HEADS].set(gdn_a_log[l]).at[1, :GDN_HEADS].set(gdn_dt_bias[l])
        pcol = jnp.zeros((8, LANES), F32).at[:GDN_HEADS, 0].set(gdn_a_log[l]).at[:GDN_HEADS, 1].set(gdn_dt_bias[l])
        nw2 = gdn_norm_w[l].reshape(1, GDN_DV)
        subw2 = jnp.tile(da_subln_w[l], LANES // DA_V).reshape(1, LANES)
        wr = jnp.pad(w_router[l], ((0, 0), (0, LANES - N_EXPERTS))).astype(BF16)
        br = jnp.pad(b_router[l], (0, LANES - N_EXPERTS)).reshape(1, LANES)
        bgu = jnp.concatenate([b_gu[l][..., 0::2], b_gu[l][..., 1::2]], axis=-1).reshape(N_EXPERTS, 1, 2 * D_FF)
        bd = b_down[l].reshape(N_EXPERTS, 1, d)

        sbq, sbk, sbv, g4, daq, dak, dav, gab, gabt = _inproj_call(
            x2, sc1, sh1, w_main, w_ab, w_abt, cos_t, sinm_t, sinp_t, seq=seq, tm=tm)
        o_sb = _sb_call(sbq, sbk, sbv, batch=batch, seq=seq, tq=tq)
        o_gdn = _gdn_call(g4, gab, gabt, conv_w[l], prow, pcol, nw2, batch=batch, seq=seq, tt=tt)
        o_da = _da_call(da_lambda[l], daq, dak, dav, subw2, batch=batch, seq=seq, tq=tq, lambda_init=lambda_init)
        x1, h2, route = _outproj_call(
            x2, o_sb, o_gdn, o_da, w_out[l].astype(BF16), g1, ln1_g[l].reshape(1, d), ln1_b[l].reshape(1, d),
            sc2, sh2, wr, br, seq=seq, tm=tm, dn_alpha=dn_alpha)
        block_expert, nvalid, row_tok3, pos = _routing(route, n)
        y_rows = _moe_call(block_expert, nvalid, row_tok3, h2, wgu_all, bgu, wd_all, bd, layer=l)
        pos3 = pos.reshape(n // tc, tc, TOP_K).transpose(0, 2, 1).reshape(n // tc, 1, TOP_K * tc)
        x2 = _combine_call(pos3, y_rows, x1, route, g2, ln2_g[l].reshape(1, d), ln2_b[l].reshape(1, d),
                           seq=seq, tm=tc, dn_alpha=dn_alpha)
    return x2.reshape(batch, seq, d)
```
